```python
import jax
import jax.numpy as jnp
from jax import lax
import numpy as np

D_MODEL = 1024
BATCH = 1
SEQ = 16384
DEPTH = 2
DEC_BATCH = 128
DEC_SEQ = 4
PAST_LEN = 16384
PAGE_SIZE = 128

N_HEADS = 8
NOPE_DIM = 64
ROPE_DIM = 32
QK_DIM = NOPE_DIM + ROPE_DIM
V_DIM = 64
ATTN_WIDTH = N_HEADS * V_DIM
Q_RANK = 384
KV_RANK = 256
ROPE_THETA = 10000.0
Q_BLOCK = 128
ATTN_SCALE = QK_DIM ** -0.5
NEG_INF = -1e30
CHUNK = 128
SG_GROUPS = 4
SG_WIDTH = 512
SG_GROUP_DIM = SG_WIDTH // SG_GROUPS
LRU_WIDTH = 1024
LRU_BLOCKS = 8
LRU_BLOCK_DIM = LRU_WIDTH // LRU_BLOCKS
CONV_WIDTH = 4
LRU_C = 8.0
D_FF = 4 * D_MODEL
EPS = 1e-6
IN_SIZES = (Q_RANK, KV_RANK, ROPE_DIM, SG_WIDTH, SG_WIDTH, LRU_WIDTH, LRU_WIDTH, D_MODEL, D_MODEL, D_MODEL)
D_IN = Q_RANK + KV_RANK + ROPE_DIM + 2 * SG_WIDTH + 2 * LRU_WIDTH + 3 * D_MODEL

kernel_name = 'hybrid_mla_sgu_rglru_step'


def rmsnorm(x, g):
    xf = x.astype(jnp.float32)
    y = xf * lax.rsqrt(jnp.mean(xf * xf, axis=-1, keepdims=True) + EPS)
    return (y * g.astype(jnp.float32)).astype(x.dtype)


def split_columns(z):
    out, off = [], 0
    for n in IN_SIZES:
        out.append(z[..., off:off + n])
        off += n
    return out


def rope_angles(pos):
    half = ROPE_DIM // 2
    inv = ROPE_THETA ** (-jnp.arange(half, dtype=jnp.float32) / half)
    ang = pos.astype(jnp.float32)[:, None] * inv[None, :]
    return jnp.cos(ang), jnp.sin(ang)


def apply_rope(x, cos, sin):
    x1, x2 = jnp.split(x, 2, axis=-1)
    c = cos.astype(x.dtype)
    s = sin.astype(x.dtype)
    return jnp.concatenate([x1 * c - x2 * s, x1 * s + x2 * c], axis=-1)


def mla_queries(cq_raw, cos, sin, g_cq, w_uq, g_qn):
    cq = rmsnorm(cq_raw, g_cq)
    q = jnp.einsum('btr,rhd->bthd', cq, w_uq)
    q_pe = apply_rope(q[..., NOPE_DIM:], cos[:, None, :], sin[:, None, :])
    q = jnp.concatenate([q[..., :NOPE_DIM], q_pe], axis=-1)
    return rmsnorm(q, g_qn)


def mla_keys_values(ckv, k_rope, w_uk, w_uv, g_kn):
    k_nope = jnp.einsum('btr,rhd->bthd', ckv, w_uk)
    k_pe = jnp.broadcast_to(k_rope[:, :, None, :], k_nope.shape[:3] + (ROPE_DIM,))
    k = rmsnorm(jnp.concatenate([k_nope, k_pe], axis=-1), g_kn)
    v = jnp.einsum('btr,rhd->bthd', ckv, w_uv)
    return k, v


def mla_attention_prompt(q, k, v):
    b, s = q.shape[0], q.shape[1]
    nb = s // Q_BLOCK
    q_blocks = q.reshape(b, nb, Q_BLOCK, N_HEADS, QK_DIM).transpose(1, 0, 2, 3, 4)
    k_pos = jnp.arange(s)

    def one_block(args):
        q_blk, blk = args
        scores = jnp.einsum('bqhd,bkhd->bhqk', q_blk, k).astype(jnp.float32) * ATTN_SCALE
        q_pos = blk * Q_BLOCK + jnp.arange(Q_BLOCK)
        scores = jnp.where(k_pos[None, :] <= q_pos[:, None], scores, NEG_INF)
        probs = jax.nn.softmax(scores, axis=-1).astype(v.dtype)
        return jnp.einsum('bhqk,bkhd->bqhd', probs, v)

    out = lax.map(one_block, (q_blocks, jnp.arange(nb)))
    return out.transpose(1, 0, 2, 3, 4).reshape(b, s, ATTN_WIDTH)


def softmax_partial(scores, v):
    m = jnp.max(scores, axis=-1)
    e = jnp.exp(scores - m[..., None])
    return m, jnp.sum(e, axis=-1), jnp.einsum('bhqk,bkhd->bhqd', e, v.astype(jnp.float32))


def mla_attention_sample(q, ckv_new, k_rope_new, cache_kv_latent, cache_k_rope, layer, page_table, w_uk, w_uv, g_kn):
    b, t = q.shape[0], q.shape[1]

    def page_block(pages):
        ckv = cache_kv_latent[layer, pages].astype(q.dtype)
        kr = cache_k_rope[layer, pages].astype(q.dtype)
        k, v = mla_keys_values(ckv, kr, w_uk, w_uv, g_kn)
        scores = jnp.einsum('bqhd,bkhd->bhqk', q, k).astype(jnp.float32) * ATTN_SCALE
        return softmax_partial(scores, v)

    m_past, l_past, a_past = lax.map(page_block, page_table.T)
    k, v = mla_keys_values(ckv_new, k_rope_new, w_uk, w_uv, g_kn)
    scores = jnp.einsum('bqhd,bkhd->bhqk', q, k).astype(jnp.float32) * ATTN_SCALE
    scores = jnp.where(jnp.tril(jnp.ones((t, t), dtype=bool)), scores, NEG_INF)
    m_new, l_new, a_new = softmax_partial(scores, v)
    m_all = jnp.concatenate([m_past, m_new[None]], axis=0)
    l_all = jnp.concatenate([l_past, l_new[None]], axis=0)
    a_all = jnp.concatenate([a_past, a_new[None]], axis=0)
    m_max = jnp.max(m_all, axis=0)
    w = jnp.exp(m_all - m_max)
    denom = jnp.sum(w * l_all, axis=0)
    numer = jnp.sum(w[..., None] * a_all, axis=0)
    out = numer / denom[..., None]
    return out.transpose(0, 2, 1, 3).reshape(b, t, ATTN_WIDTH).astype(q.dtype)


def spatial_gating(u, v, w_sp, b_sp, rows):
    b, t = u.shape[0], u.shape[1]
    nc = t // rows
    wm = jnp.tril(w_sp[:, :rows, :rows])
    vc = v.reshape(b, nc, rows, SG_GROUPS, SG_GROUP_DIM)
    mix = jnp.einsum('gts,bcsgd->bctgd', wm, vc) + b_sp[:, :rows].T[None, None, :, :, None]
    return u * mix.reshape(b, t, SG_WIDTH)


def rglru(x_rec, x_gate, conv_buf, h0, conv_w, conv_b, w_ra, b_ra, w_ix, b_ix, lru_lambda):
    b, t = x_rec.shape[0], x_rec.shape[1]
    xp = jnp.concatenate([conv_buf.astype(x_rec.dtype), x_rec], axis=1)
    xc = conv_b
    for tap in range(CONV_WIDTH):
        xc = xc + xp[:, tap:tap + t] * conv_w[tap]
    new_buf = xp[:, -(CONV_WIDTH - 1):]
    xb = xc.reshape(b, t, LRU_BLOCKS, LRU_BLOCK_DIM)
    r = jax.nn.sigmoid(jnp.einsum('btnd,nde->btne', xb, w_ra).reshape(b, t, LRU_WIDTH) + b_ra)
    i = jax.nn.sigmoid(jnp.einsum('btnd,nde->btne', xb, w_ix).reshape(b, t, LRU_WIDTH) + b_ix)
    log_a = -LRU_C * r.astype(jnp.float32) * jax.nn.softplus(-lru_lambda.astype(jnp.float32))
    a = jnp.exp(log_a)
    inp = jnp.sqrt(-jnp.expm1(2.0 * log_a)) * (i * xc).astype(jnp.float32)
    inp = inp.at[:, 0].add(a[:, 0] * h0.astype(jnp.float32))

    def combine(left, right):
        a1, b1 = left
        a2, b2 = right
        return a1 * a2, a2 * b1 + b2

    _, h = lax.associative_scan(combine, (a, inp), axis=1)
    y = h.astype(x_rec.dtype) * jax.nn.gelu(x_gate)
    return y, new_buf, h[:, -1]


def token_mixers(h, pos, p, conv_buf, h0, past):
    (cq_raw, ckv_raw, kr_raw, zu, zv, x_rec, x_gate,
     gate_a, gate_b, gate_c) = split_columns(h @ p['w_in'])
    cos, sin = rope_angles(pos)
    q = mla_queries(cq_raw, cos, sin, p['g_cq'], p['w_uq'], p['g_qn'])
    ckv = rmsnorm(ckv_raw, p['g_ckv'])
    k_rope = apply_rope(kr_raw, cos, sin)
    if past is None:
        k, v_att = mla_keys_values(ckv, k_rope, p['w_uk'], p['w_uv'], p['g_kn'])
        y_a = mla_attention_prompt(q, k, v_att)
        rows = CHUNK
    else:
        cache_kv_latent, cache_k_rope, layer, page_table = past
        y_a = mla_attention_sample(q, ckv, k_rope, cache_kv_latent, cache_k_rope, layer, page_table,
                                   p['w_uk'], p['w_uv'], p['g_kn'])
        rows = h.shape[1]
    u = jax.nn.gelu(zu)
    v_sg = jax.nn.gelu(zv)
    y_b = spatial_gating(u, v_sg, p['w_sp'], p['b_sp'], rows)
    y_c, new_buf, h_last = rglru(x_rec, x_gate, conv_buf, h0, p['conv_w'], p['conv_b'],
                                 p['w_ra'], p['b_ra'], p['w_ix'], p['b_ix'], p['lru_lambda'])
    merged = (jax.nn.sigmoid(gate_a) * (y_a @ p['w_pa'])
              + jax.nn.sigmoid(gate_b) * (y_b @ p['w_pb'])
              + jax.nn.sigmoid(gate_c) * (y_c @ p['w_pc']))
    return merged @ p['w_o'], (ckv, k_rope, v_sg, new_buf, h_last)


def trunk_layer(x, pos, p, conv_buf, h0, past):
    mix, st = token_mixers(rmsnorm(x, p['g_mix']), pos, p, conv_buf, h0, past)
    x = x + mix
    hf = rmsnorm(x, p['g_ffn'])
    x = x + jnp.square(jax.nn.relu(hf @ p['w_up'])) @ p['w_down']
    return x, st


def setup_inputs(seed: int = 0) -> dict:
    key = jax.random.key(seed)
    ks = iter(jax.random.split(key, 40))

    def nrm(shape, scale):
        return jax.random.normal(next(ks), shape, jnp.float32) * scale

    def gain(shape):
        return 1.0 + nrm(shape, 0.1)

    n_pages = PAST_LEN // PAGE_SIZE
    n_pool = (DEC_BATCH * n_pages * 5) // 4
    page_table = jax.random.permutation(next(ks), n_pool)[:DEC_BATCH * n_pages].reshape(DEC_BATCH, n_pages).astype(jnp.int32)
    a0 = jax.random.uniform(next(ks), (DEPTH, LRU_WIDTH), jnp.float32, 0.9, 0.999)
    return {
        'x_prompt': nrm((BATCH, SEQ, D_MODEL), 1.0),
        'x_sample': nrm((DEC_BATCH, DEC_SEQ, D_MODEL), 1.0),
        'cache_kv_latent': nrm((DEPTH, n_pool, PAGE_SIZE, KV_RANK), 1.0),
        'cache_k_rope': nrm((DEPTH, n_pool, PAGE_SIZE, ROPE_DIM), 1.0),
        'state_lru_h': nrm((DEPTH, DEC_BATCH, LRU_WIDTH), 0.5),
        'state_conv': nrm((DEPTH, DEC_BATCH, CONV_WIDTH - 1, LRU_WIDTH), 1.0),
        'page_table': page_table,
        'g_mix': gain((DEPTH, D_MODEL)),
        'w_in': nrm((DEPTH, D_MODEL, D_IN), D_MODEL ** -0.5),
        'g_cq': gain((DEPTH, Q_RANK)),
        'w_uq': nrm((DEPTH, Q_RANK, N_HEADS, QK_DIM), Q_RANK ** -0.5),
        'g_ckv': gain((DEPTH, KV_RANK)),
        'w_uk': nrm((DEPTH, KV_RANK, N_HEADS, NOPE_DIM), KV_RANK ** -0.5),
        'w_uv': nrm((DEPTH, KV_RANK, N_HEADS, V_DIM), KV_RANK ** -0.5),
        'g_qn': gain((DEPTH, QK_DIM)),
        'g_kn': gain((DEPTH, QK_DIM)),
        'w_sp': nrm((DEPTH, SG_GROUPS, CHUNK, CHUNK), CHUNK ** -0.5),
        'b_sp': 1.0 + nrm((DEPTH, SG_GROUPS, CHUNK), 0.1),
        'conv_w': nrm((DEPTH, CONV_WIDTH, LRU_WIDTH), CONV_WIDTH ** -0.5),
        'conv_b': nrm((DEPTH, LRU_WIDTH), 0.01),
        'w_ra': nrm((DEPTH, LRU_BLOCKS, LRU_BLOCK_DIM, LRU_BLOCK_DIM), LRU_BLOCK_DIM ** -0.5),
        'b_ra': nrm((DEPTH, LRU_WIDTH), 0.01),
        'w_ix': nrm((DEPTH, LRU_BLOCKS, LRU_BLOCK_DIM, LRU_BLOCK_DIM), LRU_BLOCK_DIM ** -0.5),
        'b_ix': nrm((DEPTH, LRU_WIDTH), 0.01),
        'lru_lambda': jnp.log(a0) - jnp.log1p(-a0),
        'w_pa': nrm((DEPTH, ATTN_WIDTH, D_MODEL), ATTN_WIDTH ** -0.5),
        'w_pb': nrm((DEPTH, SG_WIDTH, D_MODEL), SG_WIDTH ** -0.5),
        'w_pc': nrm((DEPTH, LRU_WIDTH, D_MODEL), LRU_WIDTH ** -0.5),
        'w_o': nrm((DEPTH, D_MODEL, D_MODEL), D_MODEL ** -0.5),
        'g_ffn': gain((DEPTH, D_MODEL)),
        'w_up': nrm((DEPTH, D_MODEL, D_FF), D_MODEL ** -0.5),
        'w_down': nrm((DEPTH, D_FF, D_MODEL), D_FF ** -0.5),
    }


def reference(x_prompt, x_sample, cache_kv_latent, cache_k_rope, state_lru_h, state_conv, page_table,
              g_mix, w_in, g_cq, w_uq, g_ckv, w_uk, w_uv, g_qn, g_kn, w_sp, b_sp, conv_w, conv_b,
              w_ra, b_ra, w_ix, b_ix, lru_lambda, w_pa, w_pb, w_pc, w_o, g_ffn, w_up, w_down):
    n_prompt, seq = x_prompt.shape[0], x_prompt.shape[1]
    t_dec = x_sample.shape[1]
    past_len = page_table.shape[1] * PAGE_SIZE
    pos_prompt = jnp.arange(seq)
    pos_sample = past_len + jnp.arange(t_dec)
    conv_zero = jnp.zeros((n_prompt, CONV_WIDTH - 1, LRU_WIDTH), x_prompt.dtype)
    h_zero = jnp.zeros((n_prompt, LRU_WIDTH), jnp.float32)

    yp, ys = x_prompt, x_sample
    ckv_p, kr_p, h_p, cv_p = [], [], [], []
    ckv_s, kr_s, h_s, cv_s, v_s = [], [], [], [], []
    for l in range(DEPTH):
        p = {'g_mix': g_mix[l], 'w_in': w_in[l], 'g_cq': g_cq[l], 'w_uq': w_uq[l], 'g_ckv': g_ckv[l],
             'w_uk': w_uk[l], 'w_uv': w_uv[l], 'g_qn': g_qn[l], 'g_kn': g_kn[l], 'w_sp': w_sp[l],
             'b_sp': b_sp[l], 'conv_w': conv_w[l], 'conv_b': conv_b[l], 'w_ra': w_ra[l], 'b_ra': b_ra[l],
             'w_ix': w_ix[l], 'b_ix': b_ix[l], 'lru_lambda': lru_lambda[l], 'w_pa': w_pa[l],
             'w_pb': w_pb[l], 'w_pc': w_pc[l], 'w_o': w_o[l], 'g_ffn': g_ffn[l], 'w_up': w_up[l],
             'w_down': w_down[l]}
        yp, (c_lat, k_pe, _, c_buf, h_last) = trunk_layer(yp, pos_prompt, p, conv_zero, h_zero, None)
        ckv_p.append(c_lat)
        kr_p.append(k_pe)
        cv_p.append(c_buf)
        h_p.append(h_last)
        ys, (c_lat, k_pe, v_rows, c_buf, h_last) = trunk_layer(
            ys, pos_sample, p, state_conv[l], state_lru_h[l],
            (cache_kv_latent, cache_k_rope, l, page_table))
        ckv_s.append(c_lat)
        kr_s.append(k_pe)
        cv_s.append(c_buf)
        h_s.append(h_last)
        v_s.append(v_rows)
    return (yp, ys,
            jnp.stack(ckv_p), jnp.stack(kr_p), jnp.stack(h_p), jnp.stack(cv_p),
            jnp.stack(ckv_s), jnp.stack(kr_s), jnp.stack(h_s), jnp.stack(cv_s), jnp.stack(v_s))
```

```python
import functools

import jax
import jax.numpy as jnp
import numpy as np
from jax import lax
from jax.experimental import pallas as pl
from jax.experimental.pallas import tpu as pltpu

F32, BF16 = jnp.float32, jnp.bfloat16

EPS = 1e-6
ROPE_THETA = 10000.0
LRU_C = 8.0
NEG_INF = -1e30

LANES = 128
SUBLANES = 8
VMEM_LIMIT_BYTES = 56 * 1024 * 1024

MIX_IN_ROWS = 256
MIX_OUT_ROWS = 256
ATTN_BLOCK = 512
DECODE_PAGES = 16
DECODE_SUB_PAGES = 4


def _rms(x, g):
    return x * lax.rsqrt(jnp.mean(x * x, axis=-1, keepdims=True) + EPS) * g


def _gelu(x):
    c = np.float32(np.sqrt(2.0 / np.pi))
    return 0.5 * x * (1.0 + jnp.tanh(c * (x + np.float32(0.044715) * (x * x * x))))


def _dot(a, b):
    return jnp.dot(a, b, preferred_element_type=F32)


def _dot_nt(a, b):
    return lax.dot_general(a, b, (((1,), (1,)), ((), ())), preferred_element_type=F32)


def _dot_tn(a, b):
    return lax.dot_general(a, b, (((0,), (0,)), ((), ())), preferred_element_type=F32)


def _resident(shape):
    nd = len(shape)
    return pl.BlockSpec(shape, lambda *_: (0,) * nd, pipeline_mode=pl.Buffered(1))


def _rows(tm, width):
    return pl.BlockSpec((tm, width), lambda i: (i, 0))


def _mix_in_kernel(*refs, prompt, tm, seg, sg_seg, cols, n_heads, qk_dim, rope_half, n_blocks,
                   conv_width):
    it = iter(refs)
    x_ref, rc_ref, rsa_ref, rsb_ref = next(it), next(it), next(it), next(it)
    if not prompt:
        hist_ref, h0_ref = next(it), next(it)
    gmix_ref, win_ref, gcq_ref, wuq_ref, gqn_ref, gckv_ref = (next(it) for _ in range(6))
    if prompt:
        wuk_ref, gkn_ref, wuv_ref = next(it), next(it), next(it)
    (wsp_ref, bsp_ref, convw_ref, convb_ref, wrx_ref, brx_ref, lam_ref, wpb_ref,
     wpc_ref) = (next(it) for _ in range(9))
    q_ref, ckv_ref, kr_ref = next(it), next(it), next(it)
    if prompt:
        k_ref, v_ref = next(it), next(it)
    else:
        vsg_ref, xrec_ref, hseq_ref = next(it), next(it), next(it)
    mbc_ref, siga_ref = next(it), next(it)
    if prompt:
        xlast_ref, hlast_ref = next(it), next(it)
    yb_scr = next(it)
    if prompt:
        xprev_scr, hcarry_scr = next(it), next(it)

        @pl.when(pl.program_id(0) == 0)
        def _():
            xprev_scr[...] = jnp.zeros_like(xprev_scr)
            hcarry_scr[...] = jnp.zeros_like(hcarry_scr)

    h = _rms(x_ref[...], gmix_ref[...]).astype(BF16)

    def proj(name):
        lo, hi = cols[name]
        return _dot(h, win_ref[:, lo:hi])

    rc, rsa, rsb = rc_ref[...], rsa_ref[...], rsb_ref[...]

    def rope(blk):
        return (blk * rc + pltpu.roll(blk, rope_half, 1) * rsa
                + pltpu.roll(blk, LANES - rope_half, 1) * rsb)

    def head_norm(blk, g):
        ssq = jnp.sum(blk * blk, axis=-1, keepdims=True)
        return blk * lax.rsqrt(ssq * np.float32(1.0 / qk_dim) + EPS) * g

    cq = _rms(proj("cq"), gcq_ref[...]).astype(BF16)
    qf = _dot(cq, wuq_ref[...])
    for hd in range(n_heads):
        sl = slice(hd * LANES, (hd + 1) * LANES)
        q_ref[:, sl] = head_norm(rope(qf[:, sl]), gqn_ref[...]).astype(BF16)
    ckv = _rms(proj("ckv"), gckv_ref[...])
    ckv_ref[...] = ckv
    kr = rope(proj("kr"))
    kr_ref[...] = kr
    if prompt:
        ckv_bf = ckv.astype(BF16)
        kn = _dot(ckv_bf, wuk_ref[...])
        for hd in range(n_heads):
            sl = slice(hd * LANES, (hd + 1) * LANES)
            k_ref[:, sl] = head_norm(kn[:, sl] + kr, gkn_ref[...]).astype(BF16)
        v_ref[...] = _dot(ckv_bf, wuv_ref[...]).astype(BF16)

    u = _gelu(proj("zu"))
    vs = _gelu(proj("zv"))
    if not prompt:
        vsg_ref[...] = vs
    r_i = lax.broadcasted_iota(jnp.int32, (LANES, LANES), 0)
    c_i = lax.broadcasted_iota(jnp.int32, (LANES, LANES), 1)
    keep = c_i <= r_i
    if sg_seg < LANES:
        shift = sg_seg.bit_length() - 1
        keep = keep & ((r_i >> shift) == (c_i >> shift))
    n_groups = wsp_ref.shape[0]
    wm = [jnp.where(keep, wsp_ref[g], 0.0).astype(BF16) for g in range(n_groups)]
    for c in range(tm // LANES):
        rs = slice(c * LANES, (c + 1) * LANES)
        for g in range(n_groups):
            cs = slice(g * LANES, (g + 1) * LANES)
            mix = _dot(wm[g], vs[rs, cs].astype(BF16)) + bsp_ref[:, cs]
            yb_scr[rs, cs] = (u[rs, cs] * mix).astype(BF16)

    xr = proj("xrec")
    width = xr.shape[1]
    row = lax.broadcasted_iota(jnp.int32, (tm, width), 0)
    t_idx = row & (seg - 1)
    xc = convb_ref[...] + xr * convw_ref[conv_width - 1:conv_width, :]
    for k in range(1, conv_width):
        if prompt:
            fill = pltpu.roll(xprev_scr[...], k, 0)
        else:
            fill = hist_ref[k - 1]
        xs = jnp.where(t_idx >= k, pltpu.roll(xr, k, 0), fill)
        xc = xc + xs * convw_ref[conv_width - 1 - k:conv_width - k, :]
    if prompt:
        xprev_scr[...] = xr
        xlast_ref[...] = xr[tm - SUBLANES:, :]
    else:
        xrec_ref[...] = xr

    lam = lam_ref[...]
    neg_sp = -LRU_C * (jnp.maximum(-lam, 0.0) + jnp.log1p(jnp.exp(-jnp.abs(lam))))
    a_parts, b_parts = [], []
    for n in range(n_blocks):
        sl = slice(n * LANES, (n + 1) * LANES)
        xcb = xc[:, sl]
        ri = _dot(xcb.astype(BF16), wrx_ref[n]) + brx_ref[n]
        r_gate = jax.nn.sigmoid(ri[:, :LANES])
        i_gate = jax.nn.sigmoid(ri[:, LANES:])
        log_a = r_gate * neg_sp[:, sl]
        a_blk = jnp.exp(log_a)
        a_parts.append(a_blk)
        b_parts.append(jnp.sqrt(jnp.tanh(-log_a) * (1.0 + a_blk * a_blk)) * (i_gate * xcb))
    a = jnp.concatenate(a_parts, axis=1)
    b = jnp.concatenate(b_parts, axis=1)
    if prompt:
        h0 = jnp.where(row == 0, hcarry_scr[0:1, :], 0.0)
    else:
        h0 = h0_ref[...]
    b = b + a * h0
    s = 1
    while s < seg:
        ok = t_idx >= s
        a_sh = jnp.where(ok, pltpu.roll(a, s, 0), 1.0)
        b_sh = jnp.where(ok, pltpu.roll(b, s, 0), 0.0)
        b = a * b_sh + b
        a = a * a_sh
        s *= 2
    hseq = b
    if prompt:
        hcarry_scr[...] = jnp.broadcast_to(hseq[tm - 1:tm, :], hcarry_scr.shape)
        hlast_ref[...] = hseq[tm - SUBLANES:, :]
    else:
        hseq_ref[...] = hseq
    yc = (hseq * _gelu(proj("xgate"))).astype(BF16)

    mb = jax.nn.sigmoid(proj("gb")) * _dot(yb_scr[...], wpb_ref[...])
    mc = jax.nn.sigmoid(proj("gc")) * _dot(yc, wpc_ref[...])
    mbc_ref[...] = mb + mc
    siga_ref[...] = jax.nn.sigmoid(proj("ga"))


def _mix_in(x, tabs, state, wts, *, prompt, seg, sg_seg, dims):
    rows, d_model = x.shape
    tm = min(MIX_IN_ROWS, rows)
    assert rows % tm == 0 and tm % LANES == 0
    if prompt:
        seg = tm
    assert seg & (seg - 1) == 0 and sg_seg & (sg_seg - 1) == 0 and tm % seg == 0
    H, kvr, lru_w, sg_w = dims["n_heads"], dims["kv_rank"], dims["lru_width"], dims["sg_width"]
    hp = H * LANES
    kern = functools.partial(
        _mix_in_kernel, prompt=prompt, tm=tm, seg=seg, sg_seg=sg_seg, cols=dims["cols"],
        n_heads=H, qk_dim=dims["qk_dim"], rope_half=dims["rope_dim"] // 2,
        n_blocks=dims["lru_blocks"], conv_width=dims["conv_width"])

    ins = [x, *tabs]
    in_specs = [_rows(tm, d_model)] + [_rows(tm, LANES)] * 3
    if not prompt:
        hist, h0 = state
        ins += [hist, h0]
        in_specs += [pl.BlockSpec((hist.shape[0], tm, lru_w), lambda i: (0, i, 0)), _rows(tm, lru_w)]
    names = ["g_mix", "w_in", "g_cq", "w_uq", "g_qn", "g_ckv"]
    if prompt:
        names += ["w_uk_pad", "g_kn", "w_uv"]
    names += ["w_sp", "b_sp", "conv_w", "conv_b", "w_rx", "b_rx", "lam", "w_pb", "w_pc"]
    for n in names:
        ins.append(wts[n])
        in_specs.append(_resident(wts[n].shape))

    out_shape = [jax.ShapeDtypeStruct((rows, hp), BF16),
                 jax.ShapeDtypeStruct((rows, kvr), F32),
                 jax.ShapeDtypeStruct((rows, LANES), F32)]
    out_specs = [_rows(tm, hp), _rows(tm, kvr), _rows(tm, LANES)]
    if prompt:
        out_shape += [jax.ShapeDtypeStruct((rows, hp), BF16),
                      jax.ShapeDtypeStruct((rows, H * dims["v_dim"]), BF16)]
        out_specs += [_rows(tm, hp), _rows(tm, H * dims["v_dim"])]
    else:
        out_shape += [jax.ShapeDtypeStruct((rows, sg_w), F32),
                      jax.ShapeDtypeStruct((rows, lru_w), F32),
                      jax.ShapeDtypeStruct((rows, lru_w), F32)]
        out_specs += [_rows(tm, sg_w), _rows(tm, lru_w), _rows(tm, lru_w)]
    out_shape += [jax.ShapeDtypeStruct((rows, d_model), F32)] * 2
    out_specs += [_rows(tm, d_model)] * 2
    scratch = [pltpu.VMEM((tm, sg_w), BF16)]
    if prompt:
        last = pl.BlockSpec((SUBLANES, lru_w), lambda i: (0, 0))
        out_shape += [jax.ShapeDtypeStruct((SUBLANES, lru_w), F32)] * 2
        out_specs += [last, last]
        scratch += [pltpu.VMEM((tm, lru_w), F32), pltpu.VMEM((SUBLANES, lru_w), F32)]

    return pl.pallas_call(
        kern, grid=(rows // tm,), in_specs=in_specs, out_specs=out_specs, out_shape=out_shape,
        scratch_shapes=scratch, name="mix_in_prompt" if prompt else "mix_in_sample",
        compiler_params=pltpu.CompilerParams(dimension_semantics=("arbitrary",),
                                             vmem_limit_bytes=VMEM_LIMIT_BYTES),
    )(*ins)


def _flash_kernel(qi_ref, ki_ref, q_ref, k_ref, v_ref, o_ref, m_scr, l_scr, acc_scr, *, n_heads,
                  v_dim, blk):
    step = pl.program_id(0)
    qi, ki = qi_ref[step], ki_ref[step]

    @pl.when(ki == 0)
    def _():
        m_scr[...] = jnp.full_like(m_scr, NEG_INF)
        l_scr[...] = jnp.zeros_like(l_scr)
        acc_scr[...] = jnp.zeros_like(acc_scr)

    def update(masked):
        if masked:
            r_i = lax.broadcasted_iota(jnp.int32, (blk, blk), 0)
            c_i = lax.broadcasted_iota(jnp.int32, (blk, blk), 1)
            keep = c_i <= r_i
        for hd in range(n_heads):
            sl = slice(hd * LANES, (hd + 1) * LANES)
            s = _dot_nt(q_ref[:, sl], k_ref[:, sl])
            if masked:
                s = jnp.where(keep, s, NEG_INF)
            m_prev = m_scr[hd]
            m_new = jnp.maximum(m_prev, jnp.max(s, axis=-1, keepdims=True))
            alpha = jnp.exp(m_prev - m_new)
            p = jnp.exp(s - m_new)
            l_scr[hd] = alpha * l_scr[hd] + jnp.sum(p, axis=-1, keepdims=True)
            pv = _dot(p.astype(BF16), v_ref[:, hd * v_dim:(hd + 1) * v_dim])
            acc_scr[hd] = alpha * acc_scr[hd] + pv
            m_scr[hd] = m_new

    @pl.when(ki < qi)
    def _():
        update(False)

    @pl.when(ki == qi)
    def _():
        update(True)
        for hd in range(n_heads):
            o_ref[:, hd * v_dim:(hd + 1) * v_dim] = (acc_scr[hd] / l_scr[hd]).astype(o_ref.dtype)


def _prompt_attention(q, k, v, *, n_heads, v_dim):
    rows = q.shape[0]
    blk = min(ATTN_BLOCK, rows)
    assert rows % blk == 0
    nb = rows // blk
    pairs = [(i, j) for i in range(nb) for j in range(i + 1)]
    qi = jnp.asarray([p[0] for p in pairs], jnp.int32)
    ki = jnp.asarray([p[1] for p in pairs], jnp.int32)
    grid_spec = pltpu.PrefetchScalarGridSpec(
        num_scalar_prefetch=2, grid=(len(pairs),),
        in_specs=[pl.BlockSpec((blk, q.shape[1]), lambda s, qi, ki: (qi[s], 0)),
                  pl.BlockSpec((blk, k.shape[1]), lambda s, qi, ki: (ki[s], 0)),
                  pl.BlockSpec((blk, v.shape[1]), lambda s, qi, ki: (ki[s], 0))],
        out_specs=pl.BlockSpec((blk, v.shape[1]), lambda s, qi, ki: (qi[s], 0)),
        scratch_shapes=[pltpu.VMEM((n_heads, blk, 1), F32), pltpu.VMEM((n_heads, blk, 1), F32),
                        pltpu.VMEM((n_heads, blk, v_dim), F32)])
    return pl.pallas_call(
        functools.partial(_flash_kernel, n_heads=n_heads, v_dim=v_dim, blk=blk),
        grid_spec=grid_spec, out_shape=jax.ShapeDtypeStruct((rows, v.shape[1]), BF16),
        name="prompt_attention",
        compiler_params=pltpu.CompilerParams(dimension_semantics=("arbitrary",),
                                             vmem_limit_bytes=VMEM_LIMIT_BYTES),
    )(qi, ki, q, k, v)


def _decode_kernel(pt_ref, q_ref, cnew_ref, knew_ref, gkn_ref, wukp_ref, wuk_ref, wuv_ref, e_ref,
                   ckv_hbm, kr_hbm, o_ref, ckv_buf, kr_buf, sem, qabs_scr, qpe_scr, m_scr, l_scr,
                   acc_scr, *, layer, n_pages, n_chunks, sub_pages, n_heads, n_q, nope, rope_dim,
                   qk_dim, v_dim):
    b, c = pl.program_id(0), pl.program_id(1)
    n_b = pl.num_programs(0)
    step = b * n_chunks + c
    slot = step % 2
    nqh = n_q * n_heads

    def chunk_copies(bb, cc, sl):
        out = []
        for j in range(n_pages):
            pid = pt_ref[bb, cc * n_pages + j]
            out.append(pltpu.make_async_copy(ckv_hbm.at[layer, pid], ckv_buf.at[sl, j], sem.at[0, sl]))
            out.append(pltpu.make_async_copy(kr_hbm.at[layer, pid], kr_buf.at[sl, j], sem.at[1, sl]))
        return out

    @pl.when(step == 0)
    def _():
        for d in chunk_copies(0, 0, 0):
            d.start()

    @pl.when(step + 1 < n_b * n_chunks)
    def _():
        nxt = step + 1
        for d in chunk_copies(nxt // n_chunks, nxt % n_chunks, 1 - slot):
            d.start()

    eye = (lax.broadcasted_iota(jnp.int32, (nqh, nqh), 0)
           == lax.broadcasted_iota(jnp.int32, (nqh, nqh), 1))

    def to_col(v_row):
        return jnp.sum(jnp.where(eye, v_row, 0.0), axis=1, keepdims=True)

    def scores(ckv, kpe):
        ckv_bf = ckv.astype(BF16)
        kn = _dot(ckv_bf, wuk_ref[...])
        ssq = _dot((kn * kn).astype(BF16), e_ref[...]) + jnp.sum(kpe * kpe, axis=-1, keepdims=True)
        rs = lax.rsqrt(ssq * np.float32(1.0 / qk_dim) + EPS)
        s = _dot_nt(ckv_bf, qabs_scr[...]) + _dot_nt(kpe.astype(BF16), qpe_scr[...])
        return s * rs, ckv_bf

    def accumulate(s, ckv_bf):
        m_prev = m_scr[...]
        m_new = jnp.maximum(m_prev, jnp.max(s, axis=0, keepdims=True))
        alpha = jnp.exp(m_prev - m_new)
        p = jnp.exp(s - m_new)
        l_scr[...] = alpha * l_scr[...] + jnp.sum(p, axis=0, keepdims=True)
        acc_scr[...] = to_col(alpha) * acc_scr[...] + _dot_tn(p.astype(BF16), ckv_bf)
        m_scr[...] = m_new

    @pl.when(c == 0)
    def _():
        qg = q_ref[0].astype(F32) * gkn_ref[...]
        r_i = lax.broadcasted_iota(jnp.int32, qg.shape, 0)
        c_i = lax.broadcasted_iota(jnp.int32, qg.shape, 1)
        qbd = jnp.where((c_i >> 7) == (r_i & (n_heads - 1)), qg, 0.0)
        qabs_scr[...] = _dot_nt(qbd.astype(BF16), wukp_ref[...]).astype(BF16)
        qfold = qbd[:, 0:LANES]
        for hd in range(1, n_heads):
            qfold = qfold + qbd[:, hd * LANES:(hd + 1) * LANES]
        qpe_scr[...] = qfold[:, nope:nope + rope_dim].astype(BF16)
        m_scr[...] = jnp.full_like(m_scr, NEG_INF)
        l_scr[...] = jnp.zeros_like(l_scr)
        acc_scr[...] = jnp.zeros_like(acc_scr)
        cnew = cnew_ref[0]
        s, ckv_bf = scores(cnew, knew_ref[0][:, nope:nope + rope_dim])
        j_i = lax.broadcasted_iota(jnp.int32, s.shape, 0)
        q_i = lax.broadcasted_iota(jnp.int32, s.shape, 1) >> (n_heads.bit_length() - 1)
        s = jnp.where((j_i < n_q) & (j_i <= q_i), s, NEG_INF)
        accumulate(s, ckv_bf)

    for d in chunk_copies(b, c, slot):
        d.wait()

    page = ckv_buf.shape[2]
    for j0 in range(0, n_pages, sub_pages):
        ckv = ckv_buf[slot, j0:j0 + sub_pages].reshape(sub_pages * page, ckv_buf.shape[3])
        kpe = kr_buf[slot, j0:j0 + sub_pages].reshape(sub_pages * page, kr_buf.shape[3])
        s, ckv_bf = scores(ckv, kpe)
        accumulate(s, ckv_bf)

    @pl.when(c == n_chunks - 1)
    def _():
        lat = acc_scr[...] / to_col(l_scr[...])
        yfull = _dot(lat.astype(BF16), wuv_ref[...])
        r_i = lax.broadcasted_iota(jnp.int32, yfull.shape, 0)
        c_i = lax.broadcasted_iota(jnp.int32, yfull.shape, 1)
        own = (c_i >> (v_dim.bit_length() - 1)) == (r_i & (n_heads - 1))
        y = jnp.where(own, yfull, 0.0).reshape(n_q, n_heads, yfull.shape[1]).sum(axis=1)
        o_ref[0] = y.astype(o_ref.dtype)


def _sample_attention(q_rep, cnew, knew, page_table, cache_kv, cache_kr, wts, *, layer, dims):
    n_b, n_tab = page_table.shape
    n_pages = min(DECODE_PAGES, n_tab)
    sub_pages = min(DECODE_SUB_PAGES, n_pages)
    assert n_tab % n_pages == 0 and n_pages % sub_pages == 0
    n_chunks = n_tab // n_pages
    H, n_q, v_dim = dims["n_heads"], dims["dec_seq"], dims["v_dim"]
    nqh = n_q * H
    assert H == SUBLANES and n_q <= SUBLANES and LANES == 1 << 7
    assert v_dim & (v_dim - 1) == 0
    page, kvr, rope_dim = cache_kv.shape[2], cache_kv.shape[3], cache_kr.shape[3]
    kern = functools.partial(
        _decode_kernel, layer=layer, n_pages=n_pages, n_chunks=n_chunks, sub_pages=sub_pages,
        n_heads=H, n_q=n_q, nope=dims["nope"], rope_dim=rope_dim, qk_dim=dims["qk_dim"], v_dim=v_dim)

    def per_b(shape):
        nd = len(shape)
        return pl.BlockSpec((1,) + tuple(shape[1:]), lambda b, c, pt: (b,) + (0,) * (nd - 1))

    def const(shape):
        nd = len(shape)
        return pl.BlockSpec(shape, lambda b, c, pt: (0,) * nd, pipeline_mode=pl.Buffered(1))

    weights = [wts["g_kn_full"], wts["w_uk_pad"], wts["w_uk"], wts["w_uv"], wts["e_heads"]]
    grid_spec = pltpu.PrefetchScalarGridSpec(
        num_scalar_prefetch=1, grid=(n_b, n_chunks),
        in_specs=[per_b(q_rep.shape), per_b(cnew.shape), per_b(knew.shape)]
                 + [const(w.shape) for w in weights]
                 + [pl.BlockSpec(memory_space=pl.ANY), pl.BlockSpec(memory_space=pl.ANY)],
        out_specs=pl.BlockSpec((1, n_q, H * v_dim), lambda b, c, pt: (b, 0, 0)),
        scratch_shapes=[pltpu.VMEM((2, n_pages, page, kvr), F32),
                        pltpu.VMEM((2, n_pages, page, rope_dim), F32),
                        pltpu.SemaphoreType.DMA((2, 2)),
                        pltpu.VMEM((nqh, kvr), BF16), pltpu.VMEM((nqh, rope_dim), BF16),
                        pltpu.VMEM((1, nqh), F32), pltpu.VMEM((1, nqh), F32),
                        pltpu.VMEM((nqh, kvr), F32)])
    return pl.pallas_call(
        kern, grid_spec=grid_spec,
        out_shape=jax.ShapeDtypeStruct((n_b, n_q, H * v_dim), F32), name="sample_attention",
        compiler_params=pltpu.CompilerParams(dimension_semantics=("arbitrary", "arbitrary"),
                                             vmem_limit_bytes=VMEM_LIMIT_BYTES),
    )(page_table, q_rep, cnew, knew, *weights, cache_kv, cache_kr)


def _mix_out_kernel(x_ref, ya_ref, siga_ref, mbc_ref, wpa_ref, wo_ref, gffn_ref, wup_ref, wdn_ref,
                    o_ref, *, ff_chunk):
    merged = siga_ref[...] * _dot(ya_ref[...].astype(BF16), wpa_ref[...]) + mbc_ref[...]
    x1 = x_ref[...] + _dot(merged.astype(BF16), wo_ref[...])
    hf = _rms(x1, gffn_ref[...]).astype(BF16)
    acc = x1
    for lo in range(0, wup_ref.shape[1], ff_chunk):
        up = jnp.maximum(_dot(hf, wup_ref[:, lo:lo + ff_chunk]), 0.0)
        acc = acc + _dot((up * up).astype(BF16), wdn_ref[lo:lo + ff_chunk, :])
    o_ref[...] = acc


def _mix_out(x, ya, siga, mbc, wts):
    rows, d_model = x.shape
    tm = min(MIX_OUT_ROWS, rows)
    assert rows % tm == 0
    names = ["w_pa", "w_o", "g_ffn", "w_up", "w_down"]
    ff = wts["w_up"].shape[1]
    return pl.pallas_call(
        functools.partial(_mix_out_kernel, ff_chunk=min(ff, 1024)),
        grid=(rows // tm,),
        in_specs=[_rows(tm, d_model), _rows(tm, ya.shape[1]), _rows(tm, d_model), _rows(tm, d_model)]
                 + [_resident(wts[n].shape) for n in names],
        out_specs=_rows(tm, d_model), out_shape=jax.ShapeDtypeStruct((rows, d_model), F32),
        name="mix_out",
        compiler_params=pltpu.CompilerParams(dimension_semantics=("arbitrary",),
                                             vmem_limit_bytes=VMEM_LIMIT_BYTES),
    )(x, ya, siga, mbc, *[wts[n] for n in names])


def _layer_weights(l, p, dims, sample_rows):
    H, qk, nope, rope_dim, v_dim = (dims[k] for k in ("n_heads", "qk_dim", "nope", "rope_dim", "v_dim"))
    q_rank, kvr = dims["q_rank"], dims["kv_rank"]
    d_model = p["w_in"].shape[1]
    w_in = p["w_in"][l]
    split = q_rank + kvr
    w_kr = jnp.zeros((d_model, LANES), F32).at[:, nope:nope + rope_dim].set(w_in[:, split:split + rope_dim])
    w = {}
    w["w_in"] = jnp.concatenate([w_in[:, :split], w_kr, w_in[:, split + rope_dim:]], axis=1).astype(BF16)
    w["g_mix"] = p["g_mix"][l][None]
    w["g_cq"] = p["g_cq"][l][None]
    w["g_ckv"] = p["g_ckv"][l][None]
    pad_h = ((0, 0), (0, 0), (0, LANES - qk))
    w["w_uq"] = jnp.pad(p["w_uq"][l], pad_h).reshape(q_rank, H * LANES).astype(BF16)
    scale = np.float32(qk ** -0.5)
    w["g_qn"] = jnp.pad(p["g_qn"][l] * scale, (0, LANES - qk))[None]
    g_kn = jnp.pad(p["g_kn"][l], (0, LANES - qk))[None]
    w["g_kn"] = g_kn
    w["g_kn_full"] = jnp.tile(g_kn, (1, H))
    w["w_uk_pad"] = jnp.pad(p["w_uk"][l], ((0, 0), (0, 0), (0, LANES - nope))).reshape(kvr, H * LANES).astype(BF16)
    w["w_uk"] = p["w_uk"][l].reshape(kvr, H * nope).astype(BF16)
    w["w_uv"] = p["w_uv"][l].reshape(kvr, H * v_dim).astype(BF16)
    qh_cols = jnp.arange(H * dims["dec_seq"])
    w["e_heads"] = ((jnp.arange(H * nope)[:, None] // nope) == (qh_cols[None, :] % H)).astype(BF16)
    w["conv_w"] = p["conv_w"][l]
    w["conv_b"] = p["conv_b"][l][None]
    w["w_rx"] = jnp.concatenate([p["w_ra"][l], p["w_ix"][l]], axis=2).astype(BF16)
    nb, bd = p["w_ra"].shape[1], p["w_ra"].shape[2]
    w["b_rx"] = jnp.concatenate([p["b_ra"][l].reshape(nb, 1, bd), p["b_ix"][l].reshape(nb, 1, bd)], axis=2)
    w["lam"] = p["lru_lambda"][l][None]
    for n in ("w_pa", "w_pb", "w_pc", "w_o", "w_up", "w_down"):
        w[n] = p[n][l].astype(BF16)
    w["g_ffn"] = p["g_ffn"][l][None]
    w["w_sp"] = p["w_sp"][l]
    w["b_sp"] = jnp.repeat(p["b_sp"][l].T, LANES, axis=1)
    rep = LANES // sample_rows
    w["w_sp_s"] = jnp.tile(p["w_sp"][l][:, :sample_rows, :sample_rows], (1, rep, rep))
    w["b_sp_s"] = jnp.repeat(jnp.tile(p["b_sp"][l][:, :sample_rows], (1, rep)).T, LANES, axis=1)
    return w


def _rope_tables(pos, dims):
    nope, rope_dim = dims["nope"], dims["rope_dim"]
    half = rope_dim // 2
    inv = ROPE_THETA ** (-jnp.arange(half, dtype=F32) / half)
    ang = pos.astype(F32)[:, None] * inv[None, :]
    cos, sin = jnp.cos(ang), jnp.sin(ang)
    n = pos.shape[0]
    tail = LANES - nope - rope_dim
    zero_h = jnp.zeros((n, half), F32)
    rc = jnp.concatenate([jnp.ones((n, nope), F32), cos, cos, jnp.ones((n, tail), F32)], axis=1)
    rsa = jnp.concatenate([jnp.zeros((n, nope), F32), zero_h, sin, jnp.zeros((n, tail), F32)], axis=1)
    rsb = jnp.concatenate([jnp.zeros((n, nope), F32), -sin, zero_h, jnp.zeros((n, tail), F32)], axis=1)
    return rc, rsa, rsb


def kernel(x_prompt, x_sample, cache_kv_latent, cache_k_rope, state_lru_h, state_conv, page_table,
           g_mix, w_in, g_cq, w_uq, g_ckv, w_uk, w_uv, g_qn, g_kn, w_sp, b_sp, conv_w, conv_b,
           w_ra, b_ra, w_ix, b_ix, lru_lambda, w_pa, w_pb, w_pc, w_o, g_ffn, w_up, w_down):
    p = dict(g_mix=g_mix, w_in=w_in, g_cq=g_cq, w_uq=w_uq, g_ckv=g_ckv, w_uk=w_uk, w_uv=w_uv,
             g_qn=g_qn, g_kn=g_kn, w_sp=w_sp, b_sp=b_sp, conv_w=conv_w, conv_b=conv_b, w_ra=w_ra,
             b_ra=b_ra, w_ix=w_ix, b_ix=b_ix, lru_lambda=lru_lambda, w_pa=w_pa, w_pb=w_pb, w_pc=w_pc,
             w_o=w_o, g_ffn=g_ffn, w_up=w_up, w_down=w_down)
    n_prompt, seq, d_model = x_prompt.shape
    n_dec, t_dec, _ = x_sample.shape
    depth = w_in.shape[0]
    q_rank, H, qk = w_uq.shape[1:]
    kvr, _, nope = w_uk.shape[1:]
    v_dim = w_uv.shape[3]
    rope_dim = qk - nope
    sg_w, lru_w = w_pb.shape[1], w_pc.shape[1]
    cw = conv_w.shape[1]
    page = cache_kv_latent.shape[2]
    assert n_prompt == 1 and qk <= LANES and t_dec >= cw - 1
    assert w_sp.shape[2] == LANES and w_ra.shape[2] == LANES and sg_w == w_sp.shape[1] * LANES
    assert LANES % t_dec == 0 and t_dec & (t_dec - 1) == 0

    sizes = [("cq", q_rank), ("ckv", kvr), ("kr", LANES), ("zu", sg_w), ("zv", sg_w), ("xrec", lru_w),
             ("xgate", lru_w), ("ga", d_model), ("gb", d_model), ("gc", d_model)]
    cols, off = {}, 0
    for name, n in sizes:
        cols[name] = (off, off + n)
        off += n
    dims = dict(n_heads=H, qk_dim=qk, nope=nope, rope_dim=rope_dim, v_dim=v_dim, q_rank=q_rank,
                kv_rank=kvr, lru_width=lru_w, sg_width=sg_w, lru_blocks=w_ra.shape[1], conv_width=cw,
                cols=cols, dec_seq=t_dec)

    past_len = page_table.shape[1] * page
    tabs_p = _rope_tables(jnp.arange(seq), dims)
    tabs_s = _rope_tables(jnp.tile(past_len + jnp.arange(t_dec), n_dec), dims)

    yp = x_prompt.reshape(seq, d_model)
    ys = x_sample.reshape(n_dec * t_dec, d_model)
    outs = {k: [] for k in ("ckv_p", "kr_p", "h_p", "cv_p", "ckv_s", "kr_s", "h_s", "cv_s", "v_s")}
    rope_sl = slice(nope, nope + rope_dim)
    for l in range(depth):
        w = _layer_weights(l, p, dims, t_dec)
        q, ckv, kr, k, v, mbc, siga, xlast, hlast = _mix_in(
            yp, tabs_p, None, w, prompt=True, seg=None, sg_seg=LANES, dims=dims)
        ya = _prompt_attention(q, k, v, n_heads=H, v_dim=v_dim)
        yp = _mix_out(yp, ya, siga, mbc, w)
        outs["ckv_p"].append(ckv[None])
        outs["kr_p"].append(kr[None, :, rope_sl])
        outs["h_p"].append(hlast[None, SUBLANES - 1])
        outs["cv_p"].append(xlast[None, SUBLANES - (cw - 1):])
        buf = state_conv[l]
        hist = jnp.stack([
            jnp.concatenate([buf[:, cw - 1 - kk:], jnp.zeros((n_dec, t_dec - kk, lru_w), F32)], axis=1)
            .reshape(n_dec * t_dec, lru_w) for kk in range(1, cw)])
        h0 = jnp.concatenate([state_lru_h[l][:, None], jnp.zeros((n_dec, t_dec - 1, lru_w), F32)],
                             axis=1).reshape(n_dec * t_dec, lru_w)
        ws = dict(w, w_sp=w["w_sp_s"], b_sp=w["b_sp_s"])
        q, ckv, kr, vsg, xrec, hseq, mbc, siga = _mix_in(
            ys, tabs_s, (hist, h0), ws, prompt=False, seg=t_dec, sg_seg=t_dec, dims=dims)
        q_rep = jnp.repeat(q.reshape(n_dec, t_dec, H * LANES), H, axis=1)
        pad_t = ((0, 0), (0, SUBLANES - t_dec), (0, 0))
        cnew = jnp.pad(ckv.reshape(n_dec, t_dec, kvr), pad_t)
        knew = jnp.pad(kr.reshape(n_dec, t_dec, LANES), pad_t)
        ya = _sample_attention(q_rep, cnew, knew, page_table, cache_kv_latent, cache_k_rope, w,
                               layer=l, dims=dims)
        ys = _mix_out(ys, ya.reshape(n_dec * t_dec, H * v_dim), siga, mbc, w)
        outs["ckv_s"].append(ckv.reshape(n_dec, t_dec, kvr))
        outs["kr_s"].append(kr[:, rope_sl].reshape(n_dec, t_dec, rope_dim))
        outs["h_s"].append(hseq.reshape(n_dec, t_dec, lru_w)[:, t_dec - 1])
        outs["cv_s"].append(xrec.reshape(n_dec, t_dec, lru_w)[:, t_dec - (cw - 1):])
        outs["v_s"].append(vsg.reshape(n_dec, t_dec, sg_w))

    st = {k: jnp.stack(v) for k, v in outs.items()}
    return (yp.reshape(n_prompt, seq, d_model), ys.reshape(n_dec, t_dec, d_model),
            st["ckv_p"], st["kr_p"], st["h_p"], st["cv_p"],
            st["ckv_s"], st["kr_s"], st["h_s"], st["cv_s"], st["v_s"])
```

```python
import functools

import jax
import jax.numpy as jnp
import numpy as np
from jax import lax
from jax.experimental import pallas as pl
from jax.experimental.pallas import tpu as pltpu

F32, BF16 = jnp.float32, jnp.bfloat16

EPS = 1e-6
ROPE_THETA = 10000.0
LRU_C = 8.0
NEG_INF = -1e30

LANES = 128
SUBLANES = 8
VMEM_LIMIT_BYTES = 56 * 1024 * 1024

MIX_IN_ROWS = 256
MIX_OUT_ROWS = 512
ATTN_BLOCK = 512
ATTN_ROW_CHUNK = 64
DECODE_PAGES = 16
DECODE_SUB_PAGES = 16


def _rms(x, g):
    return x * lax.rsqrt(jnp.mean(x * x, axis=-1, keepdims=True) + EPS) * g


def _gelu(x):
    c = np.float32(np.sqrt(2.0 / np.pi))
    return 0.5 * x * (1.0 + jnp.tanh(c * (x + np.float32(0.044715) * (x * x * x))))


def _dot(a, b):
    return jnp.dot(a, b, preferred_element_type=F32)


def _dot_nt(a, b):
    return lax.dot_general(a, b, (((1,), (1,)), ((), ())), preferred_element_type=F32)


def _dot_tn(a, b):
    return lax.dot_general(a, b, (((0,), (0,)), ((), ())), preferred_element_type=F32)


def _resident(shape):
    nd = len(shape)
    return pl.BlockSpec(shape, lambda *_: (0,) * nd, pipeline_mode=pl.Buffered(1))


def _rows(tm, width):
    return pl.BlockSpec((tm, width), lambda i: (i, 0))


def _mix_in_kernel(*refs, prompt, tm, seg, sg_seg, cols, n_heads, qk_dim, rope_half, n_blocks,
                   conv_width):
    it = iter(refs)
    x_ref, rc_ref, rsa_ref, rsb_ref = next(it), next(it), next(it), next(it)
    if not prompt:
        hist_ref, h0_ref = next(it), next(it)
    gmix_ref, win_ref, gcq_ref, wuq_ref, gqn_ref, gckv_ref = (next(it) for _ in range(6))
    if prompt:
        wuk_ref, gkn_ref, wuv_ref, vone_ref = next(it), next(it), next(it), next(it)
    (wsp_ref, bsp_ref, convw_ref, convb_ref, wrx_ref, brx_ref, lam_ref, wpb_ref,
     wpc_ref) = (next(it) for _ in range(9))
    q_ref, ckv_ref, kr_ref = next(it), next(it), next(it)
    if prompt:
        k_ref, v_ref = next(it), next(it)
    else:
        vsg_ref, xrec_ref, hseq_ref = next(it), next(it), next(it)
    mbc_ref, siga_ref = next(it), next(it)
    if prompt:
        xlast_ref, hlast_ref = next(it), next(it)
    yb_scr = next(it)
    if prompt:
        xprev_scr, hcarry_scr = next(it), next(it)

        @pl.when(pl.program_id(0) == 0)
        def _():
            xprev_scr[...] = jnp.zeros_like(xprev_scr)
            hcarry_scr[...] = jnp.zeros_like(hcarry_scr)

    h = _rms(x_ref[...], gmix_ref[...]).astype(BF16)

    def proj(name):
        lo, hi = cols[name]
        return _dot(h, win_ref[:, lo:hi])

    rc, rsa, rsb = rc_ref[...], rsa_ref[...], rsb_ref[...]

    def rope(blk):
        return (blk * rc + pltpu.roll(blk, rope_half, 1) * rsa
                + pltpu.roll(blk, LANES - rope_half, 1) * rsb)

    def head_norm(blk, g):
        ssq = jnp.sum(blk * blk, axis=-1, keepdims=True)
        return blk * lax.rsqrt(ssq * np.float32(1.0 / qk_dim) + EPS) * g

    cq = _rms(proj("cq"), gcq_ref[...]).astype(BF16)
    qf = _dot(cq, wuq_ref[...])
    for hd in range(n_heads):
        sl = slice(hd * LANES, (hd + 1) * LANES)
        q_ref[:, sl] = head_norm(rope(qf[:, sl]), gqn_ref[...]).astype(BF16)
    ckv = _rms(proj("ckv"), gckv_ref[...])
    ckv_ref[...] = ckv
    kr = rope(proj("kr"))
    kr_ref[...] = kr
    if prompt:
        ckv_bf = ckv.astype(BF16)
        kn = _dot(ckv_bf, wuk_ref[...])
        for hd in range(n_heads):
            sl = slice(hd * LANES, (hd + 1) * LANES)
            k_ref[:, sl] = head_norm(kn[:, sl] + kr, gkn_ref[...]).astype(BF16)
        v_ref[...] = (_dot(ckv_bf, wuv_ref[...]) + vone_ref[...]).astype(BF16)

    u = _gelu(proj("zu"))
    vs = _gelu(proj("zv"))
    if not prompt:
        vsg_ref[...] = vs
    r_i = lax.broadcasted_iota(jnp.int32, (LANES, LANES), 0)
    c_i = lax.broadcasted_iota(jnp.int32, (LANES, LANES), 1)
    keep = c_i <= r_i
    if sg_seg < LANES:
        shift = sg_seg.bit_length() - 1
        keep = keep & ((r_i >> shift) == (c_i >> shift))
    n_groups = wsp_ref.shape[0]
    wm = [jnp.where(keep, wsp_ref[g], 0.0).astype(BF16) for g in range(n_groups)]
    for c in range(tm // LANES):
        rs = slice(c * LANES, (c + 1) * LANES)
        for g in range(n_groups):
            cs = slice(g * LANES, (g + 1) * LANES)
            mix = _dot(wm[g], vs[rs, cs].astype(BF16)) + bsp_ref[:, cs]
            yb_scr[rs, cs] = (u[rs, cs] * mix).astype(BF16)

    xr = proj("xrec")
    width = xr.shape[1]
    row = lax.broadcasted_iota(jnp.int32, (tm, width), 0)
    t_idx = row & (seg - 1)
    xc = convb_ref[...] + xr * convw_ref[conv_width - 1:conv_width, :]
    for k in range(1, conv_width):
        if prompt:
            fill = pltpu.roll(xprev_scr[...], k, 0)
        else:
            fill = hist_ref[k - 1]
        xs = jnp.where(t_idx >= k, pltpu.roll(xr, k, 0), fill)
        xc = xc + xs * convw_ref[conv_width - 1 - k:conv_width - k, :]
    if prompt:
        xprev_scr[...] = xr
        xlast_ref[...] = xr[tm - SUBLANES:, :]
    else:
        xrec_ref[...] = xr

    lam = lam_ref[...]
    neg_sp = -LRU_C * (jnp.maximum(-lam, 0.0) + jnp.log1p(jnp.exp(-jnp.abs(lam))))
    a_parts, b_parts = [], []
    for n in range(n_blocks):
        sl = slice(n * LANES, (n + 1) * LANES)
        xcb = xc[:, sl]
        ri = _dot(xcb.astype(BF16), wrx_ref[n]) + brx_ref[n]
        r_gate = jax.nn.sigmoid(ri[:, :LANES])
        i_gate = jax.nn.sigmoid(ri[:, LANES:])
        log_a = r_gate * neg_sp[:, sl]
        a_blk = jnp.exp(log_a)
        a_parts.append(a_blk)
        b_parts.append(jnp.sqrt(jnp.tanh(-log_a) * (1.0 + a_blk * a_blk)) * (i_gate * xcb))
    a = jnp.concatenate(a_parts, axis=1)
    b = jnp.concatenate(b_parts, axis=1)
    if prompt:
        h0 = jnp.where(row == 0, hcarry_scr[0:1, :], 0.0)
    else:
        h0 = h0_ref[...]
    b = b + a * h0
    s = 1
    while s < seg:
        ok = t_idx >= s
        a_sh = jnp.where(ok, pltpu.roll(a, s, 0), 1.0)
        b_sh = jnp.where(ok, pltpu.roll(b, s, 0), 0.0)
        b = a * b_sh + b
        a = a * a_sh
        s *= 2
    hseq = b
    if prompt:
        hcarry_scr[...] = jnp.broadcast_to(hseq[tm - 1:tm, :], hcarry_scr.shape)
        hlast_ref[...] = hseq[tm - SUBLANES:, :]
    else:
        hseq_ref[...] = hseq
    yc = (hseq * _gelu(proj("xgate"))).astype(BF16)

    mb = jax.nn.sigmoid(proj("gb")) * _dot(yb_scr[...], wpb_ref[...])
    mc = jax.nn.sigmoid(proj("gc")) * _dot(yc, wpc_ref[...])
    mbc_ref[...] = mb + mc
    siga_ref[...] = jax.nn.sigmoid(proj("ga"))


def _mix_in(x, tabs, state, wts, *, prompt, seg, sg_seg, dims):
    rows, d_model = x.shape
    tm = min(MIX_IN_ROWS, rows)
    assert rows % tm == 0 and tm % LANES == 0
    if prompt:
        seg = tm
    assert seg & (seg - 1) == 0 and sg_seg & (sg_seg - 1) == 0 and tm % seg == 0
    H, kvr, lru_w, sg_w = dims["n_heads"], dims["kv_rank"], dims["lru_width"], dims["sg_width"]
    hp = H * LANES
    kern = functools.partial(
        _mix_in_kernel, prompt=prompt, tm=tm, seg=seg, sg_seg=sg_seg, cols=dims["cols"],
        n_heads=H, qk_dim=dims["qk_dim"], rope_half=dims["rope_dim"] // 2,
        n_blocks=dims["lru_blocks"], conv_width=dims["conv_width"])

    ins = [x, *tabs]
    in_specs = [_rows(tm, d_model)] + [_rows(tm, LANES)] * 3
    if not prompt:
        hist, h0 = state
        ins += [hist, h0]
        in_specs += [pl.BlockSpec((hist.shape[0], tm, lru_w), lambda i: (0, i, 0)), _rows(tm, lru_w)]
    names = ["g_mix", "w_in", "g_cq", "w_uq", "g_qn", "g_ckv"]
    if prompt:
        names += ["w_uk_pad", "g_kn", "w_uv_pad", "v_one"]
    names += ["w_sp", "b_sp", "conv_w", "conv_b", "w_rx", "b_rx", "lam", "w_pb", "w_pc"]
    for n in names:
        ins.append(wts[n])
        in_specs.append(_resident(wts[n].shape))

    out_shape = [jax.ShapeDtypeStruct((rows, hp), BF16),
                 jax.ShapeDtypeStruct((rows, kvr), F32),
                 jax.ShapeDtypeStruct((rows, LANES), F32)]
    out_specs = [_rows(tm, hp), _rows(tm, kvr), _rows(tm, LANES)]
    if prompt:
        out_shape += [jax.ShapeDtypeStruct((rows, hp), BF16),
                      jax.ShapeDtypeStruct((rows, hp), BF16)]
        out_specs += [_rows(tm, hp), _rows(tm, hp)]
    else:
        out_shape += [jax.ShapeDtypeStruct((rows, sg_w), F32),
                      jax.ShapeDtypeStruct((rows, lru_w), F32),
                      jax.ShapeDtypeStruct((rows, lru_w), F32)]
        out_specs += [_rows(tm, sg_w), _rows(tm, lru_w), _rows(tm, lru_w)]
    out_shape += [jax.ShapeDtypeStruct((rows, d_model), F32)] * 2
    out_specs += [_rows(tm, d_model)] * 2
    scratch = [pltpu.VMEM((tm, sg_w), BF16)]
    if prompt:
        last = pl.BlockSpec((SUBLANES, lru_w), lambda i: (0, 0))
        out_shape += [jax.ShapeDtypeStruct((SUBLANES, lru_w), F32)] * 2
        out_specs += [last, last]
        scratch += [pltpu.VMEM((tm, lru_w), F32), pltpu.VMEM((SUBLANES, lru_w), F32)]

    return pl.pallas_call(
        kern, grid=(rows // tm,), in_specs=in_specs, out_specs=out_specs, out_shape=out_shape,
        scratch_shapes=scratch, name="mix_in_prompt" if prompt else "mix_in_sample",
        compiler_params=pltpu.CompilerParams(dimension_semantics=("arbitrary",),
                                             vmem_limit_bytes=VMEM_LIMIT_BYTES),
    )(*ins)


def _flash_kernel(qi_ref, ki_ref, q_ref, k_ref, v_ref, o_ref, m_scr, acc_scr, s_scr, p_scr,
                  alpha_scr, *, n_heads, v_dim, blk, rc):
    step = pl.program_id(0)
    qi, ki = qi_ref[step], ki_ref[step]
    n_lt = blk // LANES

    @pl.when(ki == 0)
    def _():
        m_scr[...] = jnp.full_like(m_scr, NEG_INF)
        acc_scr[...] = jnp.zeros_like(acc_scr)

    def head_cols(hd):
        return slice(hd * LANES, (hd + 1) * LANES)

    def qk(hd):
        s_scr[hd % 2] = _dot_nt(q_ref[:, head_cols(hd)], k_ref[:, head_cols(hd)])

    def update(masked):
        qk(0)
        for hd in range(n_heads):
            if hd + 1 < n_heads:
                qk(hd + 1)
            sb = hd % 2
            for r0 in range(0, blk, rc):
                rows = slice(r0, r0 + rc)
                tiles = [s_scr[sb, rows, j * LANES:(j + 1) * LANES] for j in range(n_lt)]
                if masked:
                    r_i = lax.broadcasted_iota(jnp.int32, (rc, LANES), 0) + r0
                    c_i = lax.broadcasted_iota(jnp.int32, (rc, LANES), 1)
                    tiles = [jnp.where(c_i + j * LANES <= r_i, t, NEG_INF) for j, t in enumerate(tiles)]
                row_max = jnp.max(functools.reduce(jnp.maximum, tiles), axis=-1, keepdims=True)
                m_prev = m_scr[hd, rows, :]
                m_new = jnp.maximum(m_prev, jnp.broadcast_to(row_max, (rc, LANES)))
                alpha_scr[sb, rows, :] = jnp.exp2(m_prev - m_new)
                m_scr[hd, rows, :] = m_new
                for j, t in enumerate(tiles):
                    p_scr[sb, rows, j * LANES:(j + 1) * LANES] = jnp.exp2(t - m_new).astype(BF16)
            acc_scr[hd] = alpha_scr[sb] * acc_scr[hd] + _dot(p_scr[sb], v_ref[:, head_cols(hd)])

    @pl.when(ki < qi)
    def _():
        update(False)

    @pl.when(ki == qi)
    def _():
        update(True)
        for hd in range(n_heads):
            acc = acc_scr[hd]
            o_ref[:, hd * v_dim:(hd + 1) * v_dim] = (
                acc[:, :v_dim] / acc[:, v_dim:v_dim + 1]).astype(o_ref.dtype)


def _prompt_attention(q, k, v, *, n_heads, v_dim):
    rows = q.shape[0]
    blk = min(ATTN_BLOCK, rows)
    assert rows % blk == 0
    nb = rows // blk
    pairs = [(i, j) for i in range(nb) for j in range(i + 1)]
    qi = jnp.asarray([p[0] for p in pairs], jnp.int32)
    ki = jnp.asarray([p[1] for p in pairs], jnp.int32)
    grid_spec = pltpu.PrefetchScalarGridSpec(
        num_scalar_prefetch=2, grid=(len(pairs),),
        in_specs=[pl.BlockSpec((blk, q.shape[1]), lambda s, qi, ki: (qi[s], 0)),
                  pl.BlockSpec((blk, k.shape[1]), lambda s, qi, ki: (ki[s], 0)),
                  pl.BlockSpec((blk, v.shape[1]), lambda s, qi, ki: (ki[s], 0))],
        out_specs=pl.BlockSpec((blk, n_heads * v_dim), lambda s, qi, ki: (qi[s], 0)),
        scratch_shapes=[pltpu.VMEM((n_heads, blk, LANES), F32),
                        pltpu.VMEM((n_heads, blk, LANES), F32),
                        pltpu.VMEM((2, blk, blk), F32),
                        pltpu.VMEM((2, blk, blk), BF16),
                        pltpu.VMEM((2, blk, LANES), F32)])
    return pl.pallas_call(
        functools.partial(_flash_kernel, n_heads=n_heads, v_dim=v_dim, blk=blk,
                          rc=min(ATTN_ROW_CHUNK, blk)),
        grid_spec=grid_spec, out_shape=jax.ShapeDtypeStruct((rows, n_heads * v_dim), BF16),
        name="prompt_attention",
        compiler_params=pltpu.CompilerParams(dimension_semantics=("arbitrary",),
                                             vmem_limit_bytes=VMEM_LIMIT_BYTES),
    )(qi, ki, q, k, v)


def _decode_kernel(pt_ref, q_ref, cnew_ref, knew_ref, gkn_ref, wukp_ref, wuk_ref, wuv_ref,
                   ckv_hbm, kr_hbm, o_ref, ckv_buf, kr_buf, sem, qabs_scr, qpe_scr, m_scr, l_scr,
                   acc_scr, *, layer, n_pages, n_chunks, sub_pages, n_heads, n_q, nope, rope_dim,
                   qk_dim, v_dim):
    b, c = pl.program_id(0), pl.program_id(1)
    n_b = pl.num_programs(0)
    step = b * n_chunks + c
    slot = step % 2
    nqh = n_q * n_heads

    def chunk_copies(bb, cc, sl):
        out = []
        for j in range(n_pages):
            pid = pt_ref[bb, cc * n_pages + j]
            out.append(pltpu.make_async_copy(ckv_hbm.at[layer, pid], ckv_buf.at[sl, j], sem.at[0, sl]))
            out.append(pltpu.make_async_copy(kr_hbm.at[layer, pid], kr_buf.at[sl, j], sem.at[1, sl]))
        return out

    @pl.when(step == 0)
    def _():
        for d in chunk_copies(0, 0, 0):
            d.start()

    @pl.when(step + 1 < n_b * n_chunks)
    def _():
        nxt = step + 1
        for d in chunk_copies(nxt // n_chunks, nxt % n_chunks, 1 - slot):
            d.start()

    head_sel = ((lax.broadcasted_iota(jnp.int32, (nqh, LANES), 0) & (n_heads - 1))
                == (lax.broadcasted_iota(jnp.int32, (nqh, LANES), 1) & (n_heads - 1))).astype(BF16)

    def scores(ckv, kpe_t):
        ckv_bf = ckv.astype(BF16)
        kn = _dot(ckv_bf, wuk_ref[...])
        sq = kn * kn
        t = sq[:, 0:LANES]
        for j in range(1, sq.shape[1] // LANES):
            t = t + sq[:, j * LANES:(j + 1) * LANES]
        ssq = _dot_nt(head_sel, t.astype(BF16)) + jnp.sum(kpe_t * kpe_t, axis=0, keepdims=True)
        rs = lax.rsqrt(ssq * np.float32(1.0 / qk_dim) + EPS)
        s = _dot_nt(qabs_scr[...], ckv_bf) + _dot(qpe_scr[...], kpe_t.astype(BF16))
        return s * rs, ckv_bf

    def accumulate(parts):
        m_prev = m_scr[...]
        m_new = m_prev
        for s, _ in parts:
            m_new = jnp.maximum(m_new, jnp.max(s, axis=1, keepdims=True))
        alpha = jnp.exp2(m_prev - m_new)
        l_new = alpha * l_scr[...]
        acc = alpha * acc_scr[...]
        for s, ckv_bf in parts:
            p = jnp.exp2(s - m_new)
            l_new = l_new + jnp.sum(p, axis=1, keepdims=True)
            acc = acc + _dot(p.astype(BF16), ckv_bf)
        l_scr[...] = l_new
        acc_scr[...] = acc
        m_scr[...] = m_new

    @pl.when(c == 0)
    def _():
        qg = q_ref[0].astype(F32) * gkn_ref[...]
        r_i = lax.broadcasted_iota(jnp.int32, qg.shape, 0)
        c_i = lax.broadcasted_iota(jnp.int32, qg.shape, 1)
        qbd = jnp.where((c_i >> 7) == (r_i & (n_heads - 1)), qg, 0.0)
        qabs_scr[...] = _dot_nt(qbd.astype(BF16), wukp_ref[...]).astype(BF16)
        qfold = qbd[:, 0:LANES]
        for hd in range(1, n_heads):
            qfold = qfold + qbd[:, hd * LANES:(hd + 1) * LANES]
        qpe_scr[...] = qfold[:, nope:nope + rope_dim].astype(BF16)
        m_scr[...] = jnp.full_like(m_scr, NEG_INF)
        l_scr[...] = jnp.zeros_like(l_scr)
        acc_scr[...] = jnp.zeros_like(acc_scr)
        cnew = cnew_ref[0]
        cnew = jnp.concatenate([cnew, jnp.zeros((LANES - cnew.shape[0], cnew.shape[1]), F32)], axis=0)
        s, ckv_bf = scores(cnew, knew_ref[0])
        j_i = lax.broadcasted_iota(jnp.int32, s.shape, 1)
        q_i = lax.broadcasted_iota(jnp.int32, s.shape, 0) >> (n_heads.bit_length() - 1)
        s = jnp.where((j_i < n_q) & (j_i <= q_i), s, NEG_INF)
        accumulate([(s, ckv_bf)])

    for d in chunk_copies(b, c, slot):
        d.wait()

    page = ckv_buf.shape[2]
    parts = []
    for j0 in range(0, n_pages, sub_pages):
        ckv = ckv_buf[slot, j0:j0 + sub_pages].reshape(sub_pages * page, ckv_buf.shape[3])
        kpe_t = jnp.concatenate([kr_buf[slot, j] for j in range(j0, j0 + sub_pages)], axis=1)
        parts.append(scores(ckv, kpe_t))
    accumulate(parts)

    @pl.when(c == n_chunks - 1)
    def _():
        lat = acc_scr[...] / l_scr[...]
        yfull = _dot(lat.astype(BF16), wuv_ref[...])
        r_i = lax.broadcasted_iota(jnp.int32, yfull.shape, 0)
        c_i = lax.broadcasted_iota(jnp.int32, yfull.shape, 1)
        own = (c_i >> (v_dim.bit_length() - 1)) == (r_i & (n_heads - 1))
        y = jnp.where(own, yfull, 0.0).reshape(n_q, n_heads, yfull.shape[1]).sum(axis=1)
        o_ref[0] = y.astype(o_ref.dtype)


def _sample_attention(q_rep, cnew, knew, page_table, cache_kv, cache_kr, wts, *, layer, dims):
    n_b, n_tab = page_table.shape
    n_pages = min(DECODE_PAGES, n_tab)
    sub_pages = min(DECODE_SUB_PAGES, n_pages)
    assert n_tab % n_pages == 0 and n_pages % sub_pages == 0
    n_chunks = n_tab // n_pages
    H, n_q, v_dim = dims["n_heads"], dims["dec_seq"], dims["v_dim"]
    nqh = n_q * H
    assert H == SUBLANES and n_q <= SUBLANES and LANES == 1 << 7
    assert v_dim & (v_dim - 1) == 0
    page, kvr, rope_dim = cache_kv.shape[2], cache_kv.shape[3], cache_kr.shape[2]
    assert (H * dims["nope"]) % LANES == 0 and LANES % H == 0
    kern = functools.partial(
        _decode_kernel, layer=layer, n_pages=n_pages, n_chunks=n_chunks, sub_pages=sub_pages,
        n_heads=H, n_q=n_q, nope=dims["nope"], rope_dim=rope_dim, qk_dim=dims["qk_dim"], v_dim=v_dim)

    def per_b(shape):
        nd = len(shape)
        return pl.BlockSpec((1,) + tuple(shape[1:]), lambda b, c, pt: (b,) + (0,) * (nd - 1))

    def const(shape):
        nd = len(shape)
        return pl.BlockSpec(shape, lambda b, c, pt: (0,) * nd, pipeline_mode=pl.Buffered(1))

    weights = [wts["g_kn_full"], wts["w_uk_pad"], wts["w_uk_il"], wts["w_uv"]]
    grid_spec = pltpu.PrefetchScalarGridSpec(
        num_scalar_prefetch=1, grid=(n_b, n_chunks),
        in_specs=[per_b(q_rep.shape), per_b(cnew.shape), per_b(knew.shape)]
                 + [const(w.shape) for w in weights]
                 + [pl.BlockSpec(memory_space=pl.ANY), pl.BlockSpec(memory_space=pl.ANY)],
        out_specs=pl.BlockSpec((1, n_q, H * v_dim), lambda b, c, pt: (b, 0, 0)),
        scratch_shapes=[pltpu.VMEM((2, n_pages, page, kvr), F32),
                        pltpu.VMEM((2, n_pages, rope_dim, page), F32),
                        pltpu.SemaphoreType.DMA((2, 2)),
                        pltpu.VMEM((nqh, kvr), BF16), pltpu.VMEM((nqh, rope_dim), BF16),
                        pltpu.VMEM((nqh, 1), F32), pltpu.VMEM((nqh, 1), F32),
                        pltpu.VMEM((nqh, kvr), F32)])
    return pl.pallas_call(
        kern, grid_spec=grid_spec,
        out_shape=jax.ShapeDtypeStruct((n_b, n_q, H * v_dim), F32), name="sample_attention",
        compiler_params=pltpu.CompilerParams(dimension_semantics=("arbitrary", "arbitrary"),
                                             vmem_limit_bytes=VMEM_LIMIT_BYTES),
    )(page_table, q_rep, cnew, knew, *weights, cache_kv, cache_kr)


def _mix_out_kernel(x_ref, ya_ref, siga_ref, mbc_ref, wpa_ref, wo_ref, gffn_ref, wup_ref, wdn_ref,
                    o_ref, *, ff_chunk):
    merged = siga_ref[...] * _dot(ya_ref[...].astype(BF16), wpa_ref[...]) + mbc_ref[...]
    x1 = x_ref[...] + _dot(merged.astype(BF16), wo_ref[...])
    hf = _rms(x1, gffn_ref[...]).astype(BF16)
    acc = x1
    for lo in range(0, wup_ref.shape[1], ff_chunk):
        up = jnp.maximum(_dot(hf, wup_ref[:, lo:lo + ff_chunk]), 0.0)
        acc = acc + _dot((up * up).astype(BF16), wdn_ref[lo:lo + ff_chunk, :])
    o_ref[...] = acc


def _mix_out(x, ya, siga, mbc, wts):
    rows, d_model = x.shape
    tm = min(MIX_OUT_ROWS, rows)
    assert rows % tm == 0
    names = ["w_pa", "w_o", "g_ffn", "w_up", "w_down"]
    ff = wts["w_up"].shape[1]
    return pl.pallas_call(
        functools.partial(_mix_out_kernel, ff_chunk=min(ff, 1024)),
        grid=(rows // tm,),
        in_specs=[_rows(tm, d_model), _rows(tm, ya.shape[1]), _rows(tm, d_model), _rows(tm, d_model)]
                 + [_resident(wts[n].shape) for n in names],
        out_specs=_rows(tm, d_model), out_shape=jax.ShapeDtypeStruct((rows, d_model), F32),
        name="mix_out",
        compiler_params=pltpu.CompilerParams(dimension_semantics=("arbitrary",),
                                             vmem_limit_bytes=VMEM_LIMIT_BYTES),
    )(x, ya, siga, mbc, *[wts[n] for n in names])


def _layer_weights(l, p, dims, sample_rows):
    H, qk, nope, rope_dim, v_dim = (dims[k] for k in ("n_heads", "qk_dim", "nope", "rope_dim", "v_dim"))
    q_rank, kvr = dims["q_rank"], dims["kv_rank"]
    d_model = p["w_in"].shape[1]
    w_in = p["w_in"][l]
    split = q_rank + kvr
    w_kr = jnp.zeros((d_model, LANES), F32).at[:, nope:nope + rope_dim].set(w_in[:, split:split + rope_dim])
    w = {}
    w["w_in"] = jnp.concatenate([w_in[:, :split], w_kr, w_in[:, split + rope_dim:]], axis=1).astype(BF16)
    w["g_mix"] = p["g_mix"][l][None]
    w["g_cq"] = p["g_cq"][l][None]
    w["g_ckv"] = p["g_ckv"][l][None]
    pad_h = ((0, 0), (0, 0), (0, LANES - qk))
    w["w_uq"] = jnp.pad(p["w_uq"][l], pad_h).reshape(q_rank, H * LANES).astype(BF16)
    scale = np.float32(qk ** -0.5 * np.log2(np.e))
    w["g_qn"] = jnp.pad(p["g_qn"][l] * scale, (0, LANES - qk))[None]
    g_kn = jnp.pad(p["g_kn"][l], (0, LANES - qk))[None]
    w["g_kn"] = g_kn
    w["g_kn_full"] = jnp.tile(g_kn, (1, H))
    w["w_uk_pad"] = jnp.pad(p["w_uk"][l], ((0, 0), (0, 0), (0, LANES - nope))).reshape(kvr, H * LANES).astype(BF16)
    w["w_uk_il"] = jnp.swapaxes(p["w_uk"][l], 1, 2).reshape(kvr, nope * H).astype(BF16)
    w["w_uv"] = p["w_uv"][l].reshape(kvr, H * v_dim).astype(BF16)
    w["w_uv_pad"] = jnp.pad(p["w_uv"][l], ((0, 0), (0, 0), (0, LANES - v_dim))).reshape(kvr, H * LANES).astype(BF16)
    w["v_one"] = jnp.tile((jnp.arange(LANES) == v_dim).astype(F32), H)[None]
    w["conv_w"] = p["conv_w"][l]
    w["conv_b"] = p["conv_b"][l][None]
    w["w_rx"] = jnp.concatenate([p["w_ra"][l], p["w_ix"][l]], axis=2).astype(BF16)
    nb, bd = p["w_ra"].shape[1], p["w_ra"].shape[2]
    w["b_rx"] = jnp.concatenate([p["b_ra"][l].reshape(nb, 1, bd), p["b_ix"][l].reshape(nb, 1, bd)], axis=2)
    w["lam"] = p["lru_lambda"][l][None]
    for n in ("w_pa", "w_pb", "w_pc", "w_o", "w_up", "w_down"):
        w[n] = p[n][l].astype(BF16)
    w["g_ffn"] = p["g_ffn"][l][None]
    w["w_sp"] = p["w_sp"][l]
    w["b_sp"] = jnp.repeat(p["b_sp"][l].T, LANES, axis=1)
    rep = LANES // sample_rows
    w["w_sp_s"] = jnp.tile(p["w_sp"][l][:, :sample_rows, :sample_rows], (1, rep, rep))
    w["b_sp_s"] = jnp.repeat(jnp.tile(p["b_sp"][l][:, :sample_rows], (1, rep)).T, LANES, axis=1)
    return w


def _rope_tables(pos, dims):
    nope, rope_dim = dims["nope"], dims["rope_dim"]
    half = rope_dim // 2
    inv = ROPE_THETA ** (-jnp.arange(half, dtype=F32) / half)
    ang = pos.astype(F32)[:, None] * inv[None, :]
    cos, sin = jnp.cos(ang), jnp.sin(ang)
    n = pos.shape[0]
    tail = LANES - nope - rope_dim
    zero_h = jnp.zeros((n, half), F32)
    rc = jnp.concatenate([jnp.ones((n, nope), F32), cos, cos, jnp.ones((n, tail), F32)], axis=1)
    rsa = jnp.concatenate([jnp.zeros((n, nope), F32), zero_h, sin, jnp.zeros((n, tail), F32)], axis=1)
    rsb = jnp.concatenate([jnp.zeros((n, nope), F32), -sin, zero_h, jnp.zeros((n, tail), F32)], axis=1)
    return rc, rsa, rsb


def kernel(x_prompt, x_sample, cache_kv_latent, cache_k_rope, state_lru_h, state_conv, page_table,
           g_mix, w_in, g_cq, w_uq, g_ckv, w_uk, w_uv, g_qn, g_kn, w_sp, b_sp, conv_w, conv_b,
           w_ra, b_ra, w_ix, b_ix, lru_lambda, w_pa, w_pb, w_pc, w_o, g_ffn, w_up, w_down):
    p = dict(g_mix=g_mix, w_in=w_in, g_cq=g_cq, w_uq=w_uq, g_ckv=g_ckv, w_uk=w_uk, w_uv=w_uv,
             g_qn=g_qn, g_kn=g_kn, w_sp=w_sp, b_sp=b_sp, conv_w=conv_w, conv_b=conv_b, w_ra=w_ra,
             b_ra=b_ra, w_ix=w_ix, b_ix=b_ix, lru_lambda=lru_lambda, w_pa=w_pa, w_pb=w_pb, w_pc=w_pc,
             w_o=w_o, g_ffn=g_ffn, w_up=w_up, w_down=w_down)
    n_prompt, seq, d_model = x_prompt.shape
    n_dec, t_dec, _ = x_sample.shape
    depth = w_in.shape[0]
    q_rank, H, qk = w_uq.shape[1:]
    kvr, _, nope = w_uk.shape[1:]
    v_dim = w_uv.shape[3]
    rope_dim = qk - nope
    sg_w, lru_w = w_pb.shape[1], w_pc.shape[1]
    cw = conv_w.shape[1]
    page = cache_kv_latent.shape[2]
    assert n_prompt == 1 and qk <= LANES and t_dec >= cw - 1
    assert w_sp.shape[2] == LANES and w_ra.shape[2] == LANES and sg_w == w_sp.shape[1] * LANES
    assert LANES % t_dec == 0 and t_dec & (t_dec - 1) == 0

    sizes = [("cq", q_rank), ("ckv", kvr), ("kr", LANES), ("zu", sg_w), ("zv", sg_w), ("xrec", lru_w),
             ("xgate", lru_w), ("ga", d_model), ("gb", d_model), ("gc", d_model)]
    cols, off = {}, 0
    for name, n in sizes:
        cols[name] = (off, off + n)
        off += n
    dims = dict(n_heads=H, qk_dim=qk, nope=nope, rope_dim=rope_dim, v_dim=v_dim, q_rank=q_rank,
                kv_rank=kvr, lru_width=lru_w, sg_width=sg_w, lru_blocks=w_ra.shape[1], conv_width=cw,
                cols=cols, dec_seq=t_dec)

    past_len = page_table.shape[1] * page
    cache_kr_t = jnp.swapaxes(cache_k_rope, 2, 3)
    tabs_p = _rope_tables(jnp.arange(seq), dims)
    tabs_s = _rope_tables(jnp.tile(past_len + jnp.arange(t_dec), n_dec), dims)

    yp = x_prompt.reshape(seq, d_model)
    ys = x_sample.reshape(n_dec * t_dec, d_model)
    outs = {k: [] for k in ("ckv_p", "kr_p", "h_p", "cv_p", "ckv_s", "kr_s", "h_s", "cv_s", "v_s")}
    rope_sl = slice(nope, nope + rope_dim)
    for l in range(depth):
        w = _layer_weights(l, p, dims, t_dec)
        q, ckv, kr, k, v, mbc, siga, xlast, hlast = _mix_in(
            yp, tabs_p, None, w, prompt=True, seg=None, sg_seg=LANES, dims=dims)
        ya = _prompt_attention(q, k, v, n_heads=H, v_dim=v_dim)
        yp = _mix_out(yp, ya, siga, mbc, w)
        outs["ckv_p"].append(ckv[None])
        outs["kr_p"].append(kr[None, :, rope_sl])
        outs["h_p"].append(hlast[None, SUBLANES - 1])
        outs["cv_p"].append(xlast[None, SUBLANES - (cw - 1):])
        buf = state_conv[l]
        hist = jnp.stack([
            jnp.concatenate([buf[:, cw - 1 - kk:], jnp.zeros((n_dec, t_dec - kk, lru_w), F32)], axis=1)
            .reshape(n_dec * t_dec, lru_w) for kk in range(1, cw)])
        h0 = jnp.concatenate([state_lru_h[l][:, None], jnp.zeros((n_dec, t_dec - 1, lru_w), F32)],
                             axis=1).reshape(n_dec * t_dec, lru_w)
        ws = dict(w, w_sp=w["w_sp_s"], b_sp=w["b_sp_s"])
        q, ckv, kr, vsg, xrec, hseq, mbc, siga = _mix_in(
            ys, tabs_s, (hist, h0), ws, prompt=False, seg=t_dec, sg_seg=t_dec, dims=dims)
        q_rep = jnp.repeat(q.reshape(n_dec, t_dec, H * LANES), H, axis=1)
        pad_t = ((0, 0), (0, SUBLANES - t_dec), (0, 0))
        cnew = jnp.pad(ckv.reshape(n_dec, t_dec, kvr), pad_t)
        knew = jnp.pad(jnp.swapaxes(kr[:, rope_sl].reshape(n_dec, t_dec, rope_dim), 1, 2),
                       ((0, 0), (0, 0), (0, LANES - t_dec)))
        ya = _sample_attention(q_rep, cnew, knew, page_table, cache_kv_latent, cache_kr_t, w,
                               layer=l, dims=dims)
        ys = _mix_out(ys, ya.reshape(n_dec * t_dec, H * v_dim), siga, mbc, w)
        outs["ckv_s"].append(ckv.reshape(n_dec, t_dec, kvr))
        outs["kr_s"].append(kr[:, rope_sl].reshape(n_dec, t_dec, rope_dim))
        outs["h_s"].append(hseq.reshape(n_dec, t_dec, lru_w)[:, t_dec - 1])
        outs["cv_s"].append(xrec.reshape(n_dec, t_dec, lru_w)[:, t_dec - (cw - 1):])
        outs["v_s"].append(vsg.reshape(n_dec, t_dec, sg_w))

    st = {k: jnp.stack(v) for k, v in outs.items()}
    return (yp.reshape(n_prompt, seq, d_model), ys.reshape(n_dec, t_dec, d_model),
            st["ckv_p"], st["kr_p"], st["h_p"], st["cv_p"],
            st["ckv_s"], st["kr_s"], st["h_s"], st["cv_s"], st["v_s"])
```

```python
import functools

import jax
import jax.numpy as jnp
import numpy as np
from jax import lax
from jax.experimental import pallas as pl
from jax.experimental.pallas import tpu as pltpu

F32, BF16 = jnp.float32, jnp.bfloat16

EPS = 1e-6
ROPE_THETA = 10000.0
LRU_C = 8.0
NEG_INF = -1e30

LANES = 128
SUBLANES = 8
VMEM_LIMIT_BYTES = 56 * 1024 * 1024

MIX_IN_ROWS = 256
MIX_OUT_ROWS = 512
ATTN_BLOCK = 512
ATTN_ROW_CHUNK = 64
DECODE_PAGES = 128
DECODE_SUB_PAGES = 16


def _rms(x, g):
    return x * lax.rsqrt(jnp.mean(x * x, axis=-1, keepdims=True) + EPS) * g


def _gelu(x):
    c = np.float32(np.sqrt(2.0 / np.pi))
    return 0.5 * x * (1.0 + jnp.tanh(c * (x + np.float32(0.044715) * (x * x * x))))


def _dot(a, b):
    return jnp.dot(a, b, preferred_element_type=F32)


def _dot_nt(a, b):
    return lax.dot_general(a, b, (((1,), (1,)), ((), ())), preferred_element_type=F32)


def _dot_tn(a, b):
    return lax.dot_general(a, b, (((0,), (0,)), ((), ())), preferred_element_type=F32)


def _resident(shape):
    nd = len(shape)
    return pl.BlockSpec(shape, lambda *_: (0,) * nd, pipeline_mode=pl.Buffered(1))


def _rows(tm, width):
    return pl.BlockSpec((tm, width), lambda i: (i, 0))


def _mix_in_kernel(*refs, prompt, tm, seg, sg_seg, cols, n_heads, qk_dim, rope_half, n_blocks,
                   conv_width):
    it = iter(refs)
    x_ref, rc_ref, rsa_ref, rsb_ref = next(it), next(it), next(it), next(it)
    if not prompt:
        hist_ref, h0_ref = next(it), next(it)
    gmix_ref, win_ref, gcq_ref, wuq_ref, gqn_ref, gckv_ref = (next(it) for _ in range(6))
    if prompt:
        wuk_ref, gkn_ref, wuv_ref, vone_ref = next(it), next(it), next(it), next(it)
    (wsp_ref, bsp_ref, convw_ref, convb_ref, wrx_ref, brx_ref, lam_ref, wpb_ref,
     wpc_ref) = (next(it) for _ in range(9))
    q_ref, ckv_ref, kr_ref = next(it), next(it), next(it)
    if prompt:
        k_ref, v_ref = next(it), next(it)
    else:
        vsg_ref, xrec_ref, hseq_ref = next(it), next(it), next(it)
    mbc_ref, siga_ref = next(it), next(it)
    if prompt:
        xlast_ref, hlast_ref = next(it), next(it)
    yb_scr = next(it)
    if prompt:
        xprev_scr, hcarry_scr = next(it), next(it)

        @pl.when(pl.program_id(0) == 0)
        def _():
            xprev_scr[...] = jnp.zeros_like(xprev_scr)
            hcarry_scr[...] = jnp.zeros_like(hcarry_scr)

    h = _rms(x_ref[...], gmix_ref[...]).astype(BF16)

    def proj(name):
        lo, hi = cols[name]
        return _dot(h, win_ref[:, lo:hi])

    rc, rsa, rsb = rc_ref[...], rsa_ref[...], rsb_ref[...]

    def rope(blk):
        return (blk * rc + pltpu.roll(blk, rope_half, 1) * rsa
                + pltpu.roll(blk, LANES - rope_half, 1) * rsb)

    def head_norm(blk, g):
        ssq = jnp.sum(blk * blk, axis=-1, keepdims=True)
        return blk * lax.rsqrt(ssq * np.float32(1.0 / qk_dim) + EPS) * g

    cq = _rms(proj("cq"), gcq_ref[...]).astype(BF16)
    qf = _dot(cq, wuq_ref[...])
    for hd in range(n_heads):
        sl = slice(hd * LANES, (hd + 1) * LANES)
        q_ref[:, sl] = head_norm(rope(qf[:, sl]), gqn_ref[...]).astype(BF16)
    ckv = _rms(proj("ckv"), gckv_ref[...])
    ckv_ref[...] = ckv
    kr = rope(proj("kr"))
    kr_ref[...] = kr
    if prompt:
        ckv_bf = ckv.astype(BF16)
        kn = _dot(ckv_bf, wuk_ref[...])
        for hd in range(n_heads):
            sl = slice(hd * LANES, (hd + 1) * LANES)
            k_ref[:, sl] = head_norm(kn[:, sl] + kr, gkn_ref[...]).astype(BF16)
        v_ref[...] = (_dot(ckv_bf, wuv_ref[...]) + vone_ref[...]).astype(BF16)

    u = _gelu(proj("zu"))
    vs = _gelu(proj("zv"))
    if not prompt:
        vsg_ref[...] = vs
    r_i = lax.broadcasted_iota(jnp.int32, (LANES, LANES), 0)
    c_i = lax.broadcasted_iota(jnp.int32, (LANES, LANES), 1)
    keep = c_i <= r_i
    if sg_seg < LANES:
        shift = sg_seg.bit_length() - 1
        keep = keep & ((r_i >> shift) == (c_i >> shift))
    n_groups = wsp_ref.shape[0]
    wm = [jnp.where(keep, wsp_ref[g], 0.0).astype(BF16) for g in range(n_groups)]
    for c in range(tm // LANES):
        rs = slice(c * LANES, (c + 1) * LANES)
        for g in range(n_groups):
            cs = slice(g * LANES, (g + 1) * LANES)
            mix = _dot(wm[g], vs[rs, cs].astype(BF16)) + bsp_ref[:, cs]
            yb_scr[rs, cs] = (u[rs, cs] * mix).astype(BF16)

    xr = proj("xrec")
    width = xr.shape[1]
    row = lax.broadcasted_iota(jnp.int32, (tm, width), 0)
    t_idx = row & (seg - 1)
    xc = convb_ref[...] + xr * convw_ref[conv_width - 1:conv_width, :]
    for k in range(1, conv_width):
        if prompt:
            fill = pltpu.roll(xprev_scr[...], k, 0)
        else:
            fill = hist_ref[k - 1]
        xs = jnp.where(t_idx >= k, pltpu.roll(xr, k, 0), fill)
        xc = xc + xs * convw_ref[conv_width - 1 - k:conv_width - k, :]
    if prompt:
        xprev_scr[...] = xr
        xlast_ref[...] = xr[tm - SUBLANES:, :]
    else:
        xrec_ref[...] = xr

    lam = lam_ref[...]
    neg_sp = -LRU_C * (jnp.maximum(-lam, 0.0) + jnp.log1p(jnp.exp(-jnp.abs(lam))))
    a_parts, b_parts = [], []
    for n in range(n_blocks):
        sl = slice(n * LANES, (n + 1) * LANES)
        xcb = xc[:, sl]
        ri = _dot(xcb.astype(BF16), wrx_ref[n]) + brx_ref[n]
        r_gate = jax.nn.sigmoid(ri[:, :LANES])
        i_gate = jax.nn.sigmoid(ri[:, LANES:])
        log_a = r_gate * neg_sp[:, sl]
        a_blk = jnp.exp(log_a)
        a_parts.append(a_blk)
        b_parts.append(jnp.sqrt(jnp.tanh(-log_a) * (1.0 + a_blk * a_blk)) * (i_gate * xcb))
    a = jnp.concatenate(a_parts, axis=1)
    b = jnp.concatenate(b_parts, axis=1)
    if prompt:
        h0 = jnp.where(row == 0, hcarry_scr[0:1, :], 0.0)
    else:
        h0 = h0_ref[...]
    b = b + a * h0
    s = 1
    while s < seg:
        ok = t_idx >= s
        a_sh = jnp.where(ok, pltpu.roll(a, s, 0), 1.0)
        b_sh = jnp.where(ok, pltpu.roll(b, s, 0), 0.0)
        b = a * b_sh + b
        a = a * a_sh
        s *= 2
    hseq = b
    if prompt:
        hcarry_scr[...] = jnp.broadcast_to(hseq[tm - 1:tm, :], hcarry_scr.shape)
        hlast_ref[...] = hseq[tm - SUBLANES:, :]
    else:
        hseq_ref[...] = hseq
    yc = (hseq * _gelu(proj("xgate"))).astype(BF16)

    mb = jax.nn.sigmoid(proj("gb")) * _dot(yb_scr[...], wpb_ref[...])
    mc = jax.nn.sigmoid(proj("gc")) * _dot(yc, wpc_ref[...])
    mbc_ref[...] = mb + mc
    siga_ref[...] = jax.nn.sigmoid(proj("ga"))


def _mix_in(x, tabs, state, wts, *, prompt, seg, sg_seg, dims):
    rows, d_model = x.shape
    tm = min(MIX_IN_ROWS, rows)
    assert rows % tm == 0 and tm % LANES == 0
    if prompt:
        seg = tm
    assert seg & (seg - 1) == 0 and sg_seg & (sg_seg - 1) == 0 and tm % seg == 0
    H, kvr, lru_w, sg_w = dims["n_heads"], dims["kv_rank"], dims["lru_width"], dims["sg_width"]
    hp = H * LANES
    kern = functools.partial(
        _mix_in_kernel, prompt=prompt, tm=tm, seg=seg, sg_seg=sg_seg, cols=dims["cols"],
        n_heads=H, qk_dim=dims["qk_dim"], rope_half=dims["rope_dim"] // 2,
        n_blocks=dims["lru_blocks"], conv_width=dims["conv_width"])

    ins = [x, *tabs]
    in_specs = [_rows(tm, d_model)] + [_rows(tm, LANES)] * 3
    if not prompt:
        hist, h0 = state
        ins += [hist, h0]
        in_specs += [pl.BlockSpec((hist.shape[0], tm, lru_w), lambda i: (0, i, 0)), _rows(tm, lru_w)]
    names = ["g_mix", "w_in", "g_cq", "w_uq", "g_qn", "g_ckv"]
    if prompt:
        names += ["w_uk_pad", "g_kn", "w_uv_pad", "v_one"]
    names += ["w_sp", "b_sp", "conv_w", "conv_b", "w_rx", "b_rx", "lam", "w_pb", "w_pc"]
    for n in names:
        ins.append(wts[n])
        in_specs.append(_resident(wts[n].shape))

    out_shape = [jax.ShapeDtypeStruct((rows, hp), BF16),
                 jax.ShapeDtypeStruct((rows, kvr), F32),
                 jax.ShapeDtypeStruct((rows, LANES), F32)]
    out_specs = [_rows(tm, hp), _rows(tm, kvr), _rows(tm, LANES)]
    if prompt:
        out_shape += [jax.ShapeDtypeStruct((rows, hp), BF16),
                      jax.ShapeDtypeStruct((rows, hp), BF16)]
        out_specs += [_rows(tm, hp), _rows(tm, hp)]
    else:
        out_shape += [jax.ShapeDtypeStruct((rows, sg_w), F32),
                      jax.ShapeDtypeStruct((rows, lru_w), F32),
                      jax.ShapeDtypeStruct((rows, lru_w), F32)]
        out_specs += [_rows(tm, sg_w), _rows(tm, lru_w), _rows(tm, lru_w)]
    out_shape += [jax.ShapeDtypeStruct((rows, d_model), F32)] * 2
    out_specs += [_rows(tm, d_model)] * 2
    scratch = [pltpu.VMEM((tm, sg_w), BF16)]
    if prompt:
        last = pl.BlockSpec((SUBLANES, lru_w), lambda i: (0, 0))
        out_shape += [jax.ShapeDtypeStruct((SUBLANES, lru_w), F32)] * 2
        out_specs += [last, last]
        scratch += [pltpu.VMEM((tm, lru_w), F32), pltpu.VMEM((SUBLANES, lru_w), F32)]

    return pl.pallas_call(
        kern, grid=(rows // tm,), in_specs=in_specs, out_specs=out_specs, out_shape=out_shape,
        scratch_shapes=scratch, name="mix_in_prompt" if prompt else "mix_in_sample",
        compiler_params=pltpu.CompilerParams(dimension_semantics=("arbitrary",),
                                             vmem_limit_bytes=VMEM_LIMIT_BYTES),
    )(*ins)


def _flash_kernel(qi_ref, ki_ref, q_ref, k_ref, v_ref, o_ref, m_scr, acc_scr, s_scr, p_scr,
                  alpha_scr, *, n_heads, v_dim, blk, rc):
    step = pl.program_id(0)
    qi, ki = qi_ref[step], ki_ref[step]
    n_lt = blk // LANES

    @pl.when(ki == 0)
    def _():
        m_scr[...] = jnp.full_like(m_scr, NEG_INF)
        acc_scr[...] = jnp.zeros_like(acc_scr)

    def head_cols(hd):
        return slice(hd * LANES, (hd + 1) * LANES)

    def qk(hd):
        s_scr[hd % 2] = _dot_nt(q_ref[:, head_cols(hd)], k_ref[:, head_cols(hd)])

    def update(masked):
        qk(0)
        for hd in range(n_heads):
            if hd + 1 < n_heads:
                qk(hd + 1)
            sb = hd % 2
            for r0 in range(0, blk, rc):
                rows = slice(r0, r0 + rc)
                tiles = [s_scr[sb, rows, j * LANES:(j + 1) * LANES] for j in range(n_lt)]
                if masked:
                    r_i = lax.broadcasted_iota(jnp.int32, (rc, LANES), 0) + r0
                    c_i = lax.broadcasted_iota(jnp.int32, (rc, LANES), 1)
                    tiles = [jnp.where(c_i + j * LANES <= r_i, t, NEG_INF) for j, t in enumerate(tiles)]
                row_max = jnp.max(functools.reduce(jnp.maximum, tiles), axis=-1, keepdims=True)
                m_prev = m_scr[hd, rows, :]
                m_new = jnp.maximum(m_prev, jnp.broadcast_to(row_max, (rc, LANES)))
                alpha_scr[sb, rows, :] = jnp.exp2(m_prev - m_new)
                m_scr[hd, rows, :] = m_new
                for j, t in enumerate(tiles):
                    p_scr[sb, rows, j * LANES:(j + 1) * LANES] = jnp.exp2(t - m_new).astype(BF16)
            acc_scr[hd] = alpha_scr[sb] * acc_scr[hd] + _dot(p_scr[sb], v_ref[:, head_cols(hd)])

    @pl.when(ki < qi)
    def _():
        update(False)

    @pl.when(ki == qi)
    def _():
        update(True)
        for hd in range(n_heads):
            acc = acc_scr[hd]
            o_ref[:, hd * v_dim:(hd + 1) * v_dim] = (
                acc[:, :v_dim] / acc[:, v_dim:v_dim + 1]).astype(o_ref.dtype)


def _prompt_attention(q, k, v, *, n_heads, v_dim):
    rows = q.shape[0]
    blk = min(ATTN_BLOCK, rows)
    assert rows % blk == 0
    nb = rows // blk
    pairs = [(i, j) for i in range(nb) for j in range(i + 1)]
    qi = jnp.asarray([p[0] for p in pairs], jnp.int32)
    ki = jnp.asarray([p[1] for p in pairs], jnp.int32)
    grid_spec = pltpu.PrefetchScalarGridSpec(
        num_scalar_prefetch=2, grid=(len(pairs),),
        in_specs=[pl.BlockSpec((blk, q.shape[1]), lambda s, qi, ki: (qi[s], 0)),
                  pl.BlockSpec((blk, k.shape[1]), lambda s, qi, ki: (ki[s], 0)),
                  pl.BlockSpec((blk, v.shape[1]), lambda s, qi, ki: (ki[s], 0))],
        out_specs=pl.BlockSpec((blk, n_heads * v_dim), lambda s, qi, ki: (qi[s], 0)),
        scratch_shapes=[pltpu.VMEM((n_heads, blk, LANES), F32),
                        pltpu.VMEM((n_heads, blk, LANES), F32),
                        pltpu.VMEM((2, blk, blk), F32),
                        pltpu.VMEM((2, blk, blk), BF16),
                        pltpu.VMEM((2, blk, LANES), F32)])
    return pl.pallas_call(
        functools.partial(_flash_kernel, n_heads=n_heads, v_dim=v_dim, blk=blk,
                          rc=min(ATTN_ROW_CHUNK, blk)),
        grid_spec=grid_spec, out_shape=jax.ShapeDtypeStruct((rows, n_heads * v_dim), BF16),
        name="prompt_attention",
        compiler_params=pltpu.CompilerParams(dimension_semantics=("arbitrary",),
                                             vmem_limit_bytes=VMEM_LIMIT_BYTES),
    )(qi, ki, q, k, v)


def _decode_kernel(pt_ref, q_ref, cnew_ref, knew_ref, gkn_ref, wukp_ref, wuk_ref, wuv_ref,
                   ckv_hbm, kr_hbm, o_ref, ckv_buf, kr_buf, sem, qabs_scr, qpe_scr, m_scr, l_scr,
                   acc_scr, *, layer, n_pages, n_chunks, sub_pages, n_heads, n_q, nope, rope_dim,
                   qk_dim, v_dim):
    b, c = pl.program_id(0), pl.program_id(1)
    n_b = pl.num_programs(0)
    step = b * n_chunks + c
    slot = step % 2
    nqh = n_q * n_heads

    def chunk_copies(bb, cc, sl):
        out = []
        for j in range(n_pages):
            pid = pt_ref[bb, cc * n_pages + j]
            out.append(pltpu.make_async_copy(ckv_hbm.at[layer, pid], ckv_buf.at[sl, j], sem.at[0, sl]))
            out.append(pltpu.make_async_copy(kr_hbm.at[layer, pid], kr_buf.at[sl, j], sem.at[1, sl]))
        return out

    @pl.when(step == 0)
    def _():
        for d in chunk_copies(0, 0, 0):
            d.start()

    @pl.when(step + 1 < n_b * n_chunks)
    def _():
        nxt = step + 1
        for d in chunk_copies(nxt // n_chunks, nxt % n_chunks, 1 - slot):
            d.start()

    head_sel = ((lax.broadcasted_iota(jnp.int32, (nqh, LANES), 0) & (n_heads - 1))
                == (lax.broadcasted_iota(jnp.int32, (nqh, LANES), 1) & (n_heads - 1))).astype(BF16)

    def scores(ckv, kpe_t):
        ckv_bf = ckv.astype(BF16)
        kn = _dot(ckv_bf, wuk_ref[...])
        sq = kn * kn
        t = sq[:, 0:LANES]
        for j in range(1, sq.shape[1] // LANES):
            t = t + sq[:, j * LANES:(j + 1) * LANES]
        ssq = _dot_nt(head_sel, t.astype(BF16)) + jnp.sum(kpe_t * kpe_t, axis=0, keepdims=True)
        rs = lax.rsqrt(ssq * np.float32(1.0 / qk_dim) + EPS)
        s = _dot_nt(qabs_scr[...], ckv_bf) + _dot(qpe_scr[...], kpe_t.astype(BF16))
        return s * rs, ckv_bf

    def accumulate(parts):
        m_prev = m_scr[...]
        m_new = m_prev
        for s, _ in parts:
            m_new = jnp.maximum(m_new, jnp.max(s, axis=1, keepdims=True))
        alpha = jnp.exp2(m_prev - m_new)
        l_new = alpha * l_scr[...]
        acc = alpha * acc_scr[...]
        for s, ckv_bf in parts:
            p = jnp.exp2(s - m_new)
            l_new = l_new + jnp.sum(p, axis=1, keepdims=True)
            acc = acc + _dot(p.astype(BF16), ckv_bf)
        l_scr[...] = l_new
        acc_scr[...] = acc
        m_scr[...] = m_new

    @pl.when(c == 0)
    def _():
        qg = q_ref[0].astype(F32) * gkn_ref[...]
        r_i = lax.broadcasted_iota(jnp.int32, qg.shape, 0)
        c_i = lax.broadcasted_iota(jnp.int32, qg.shape, 1)
        qbd = jnp.where((c_i >> 7) == (r_i & (n_heads - 1)), qg, 0.0)
        qabs_scr[...] = _dot_nt(qbd.astype(BF16), wukp_ref[...]).astype(BF16)
        qfold = qbd[:, 0:LANES]
        for hd in range(1, n_heads):
            qfold = qfold + qbd[:, hd * LANES:(hd + 1) * LANES]
        qpe_scr[...] = qfold[:, nope:nope + rope_dim].astype(BF16)
        m_scr[...] = jnp.full_like(m_scr, NEG_INF)
        l_scr[...] = jnp.zeros_like(l_scr)
        acc_scr[...] = jnp.zeros_like(acc_scr)
        cnew = cnew_ref[0]
        cnew = jnp.concatenate([cnew, jnp.zeros((LANES - cnew.shape[0], cnew.shape[1]), F32)], axis=0)
        s, ckv_bf = scores(cnew, knew_ref[0])
        j_i = lax.broadcasted_iota(jnp.int32, s.shape, 1)
        q_i = lax.broadcasted_iota(jnp.int32, s.shape, 0) >> (n_heads.bit_length() - 1)
        s = jnp.where((j_i < n_q) & (j_i <= q_i), s, NEG_INF)
        accumulate([(s, ckv_bf)])

    for d in chunk_copies(b, c, slot):
        d.wait()

    page = ckv_buf.shape[2]
    parts = []
    for j0 in range(0, n_pages, sub_pages):
        ckv = ckv_buf[slot, j0:j0 + sub_pages].reshape(sub_pages * page, ckv_buf.shape[3])
        kpe_t = jnp.concatenate([kr_buf[slot, j] for j in range(j0, j0 + sub_pages)], axis=1)
        parts.append(scores(ckv, kpe_t))
    accumulate(parts)

    @pl.when(c == n_chunks - 1)
    def _():
        lat = acc_scr[...] / l_scr[...]
        yfull = _dot(lat.astype(BF16), wuv_ref[...])
        r_i = lax.broadcasted_iota(jnp.int32, yfull.shape, 0)
        c_i = lax.broadcasted_iota(jnp.int32, yfull.shape, 1)
        own = (c_i >> (v_dim.bit_length() - 1)) == (r_i & (n_heads - 1))
        y = jnp.where(own, yfull, 0.0).reshape(n_q, n_heads, yfull.shape[1]).sum(axis=1)
        o_ref[0] = y.astype(o_ref.dtype)


def _sample_attention(q_rep, cnew, knew, page_table, cache_kv, cache_kr, wts, *, layer, dims):
    n_b, n_tab = page_table.shape
    n_pages = min(DECODE_PAGES, n_tab)
    sub_pages = min(DECODE_SUB_PAGES, n_pages)
    assert n_tab % n_pages == 0 and n_pages % sub_pages == 0
    n_chunks = n_tab // n_pages
    H, n_q, v_dim = dims["n_heads"], dims["dec_seq"], dims["v_dim"]
    nqh = n_q * H
    assert H == SUBLANES and n_q <= SUBLANES and LANES == 1 << 7
    assert v_dim & (v_dim - 1) == 0
    page, kvr, rope_dim = cache_kv.shape[2], cache_kv.shape[3], cache_kr.shape[2]
    assert (H * dims["nope"]) % LANES == 0 and LANES % H == 0
    kern = functools.partial(
        _decode_kernel, layer=layer, n_pages=n_pages, n_chunks=n_chunks, sub_pages=sub_pages,
        n_heads=H, n_q=n_q, nope=dims["nope"], rope_dim=rope_dim, qk_dim=dims["qk_dim"], v_dim=v_dim)

    def per_b(shape):
        nd = len(shape)
        return pl.BlockSpec((1,) + tuple(shape[1:]), lambda b, c, pt: (b,) + (0,) * (nd - 1))

    def const(shape):
        nd = len(shape)
        return pl.BlockSpec(shape, lambda b, c, pt: (0,) * nd, pipeline_mode=pl.Buffered(1))

    weights = [wts["g_kn_full"], wts["w_uk_pad"], wts["w_uk_il"], wts["w_uv"]]
    grid_spec = pltpu.PrefetchScalarGridSpec(
        num_scalar_prefetch=1, grid=(n_b, n_chunks),
        in_specs=[per_b(q_rep.shape), per_b(cnew.shape), per_b(knew.shape)]
                 + [const(w.shape) for w in weights]
                 + [pl.BlockSpec(memory_space=pl.ANY), pl.BlockSpec(memory_space=pl.ANY)],
        out_specs=pl.BlockSpec((1, n_q, H * v_dim), lambda b, c, pt: (b, 0, 0)),
        scratch_shapes=[pltpu.VMEM((2, n_pages, page, kvr), F32),
                        pltpu.VMEM((2, n_pages, rope_dim, page), F32),
                        pltpu.SemaphoreType.DMA((2, 2)),
                        pltpu.VMEM((nqh, kvr), BF16), pltpu.VMEM((nqh, rope_dim), BF16),
                        pltpu.VMEM((nqh, 1), F32), pltpu.VMEM((nqh, 1), F32),
                        pltpu.VMEM((nqh, kvr), F32)])
    return pl.pallas_call(
        kern, grid_spec=grid_spec,
        out_shape=jax.ShapeDtypeStruct((n_b, n_q, H * v_dim), F32), name="sample_attention",
        compiler_params=pltpu.CompilerParams(dimension_semantics=("arbitrary", "arbitrary"),
                                             vmem_limit_bytes=VMEM_LIMIT_BYTES),
    )(page_table, q_rep, cnew, knew, *weights, cache_kv, cache_kr)


def _mix_out_kernel(x_ref, ya_ref, siga_ref, mbc_ref, wpa_ref, wo_ref, gffn_ref, wup_ref, wdn_ref,
                    o_ref, *, ff_chunk):
    merged = siga_ref[...] * _dot(ya_ref[...].astype(BF16), wpa_ref[...]) + mbc_ref[...]
    x1 = x_ref[...] + _dot(merged.astype(BF16), wo_ref[...])
    hf = _rms(x1, gffn_ref[...]).astype(BF16)
    acc = x1
    for lo in range(0, wup_ref.shape[1], ff_chunk):
        up = jnp.maximum(_dot(hf, wup_ref[:, lo:lo + ff_chunk]), 0.0)
        acc = acc + _dot((up * up).astype(BF16), wdn_ref[lo:lo + ff_chunk, :])
    o_ref[...] = acc


def _mix_out(x, ya, siga, mbc, wts):
    rows, d_model = x.shape
    tm = min(MIX_OUT_ROWS, rows)
    assert rows % tm == 0
    names = ["w_pa", "w_o", "g_ffn", "w_up", "w_down"]
    ff = wts["w_up"].shape[1]
    return pl.pallas_call(
        functools.partial(_mix_out_kernel, ff_chunk=min(ff, 1024)),
        grid=(rows // tm,),
        in_specs=[_rows(tm, d_model), _rows(tm, ya.shape[1]), _rows(tm, d_model), _rows(tm, d_model)]
                 + [_resident(wts[n].shape) for n in names],
        out_specs=_rows(tm, d_model), out_shape=jax.ShapeDtypeStruct((rows, d_model), F32),
        name="mix_out",
        compiler_params=pltpu.CompilerParams(dimension_semantics=("arbitrary",),
                                             vmem_limit_bytes=VMEM_LIMIT_BYTES),
    )(x, ya, siga, mbc, *[wts[n] for n in names])


def _layer_weights(l, p, dims, sample_rows):
    H, qk, nope, rope_dim, v_dim = (dims[k] for k in ("n_heads", "qk_dim", "nope", "rope_dim", "v_dim"))
    q_rank, kvr = dims["q_rank"], dims["kv_rank"]
    d_model = p["w_in"].shape[1]
    w_in = p["w_in"][l]
    split = q_rank + kvr
    w_kr = jnp.zeros((d_model, LANES), F32).at[:, nope:nope + rope_dim].set(w_in[:, split:split + rope_dim])
    w = {}
    w["w_in"] = jnp.concatenate([w_in[:, :split], w_kr, w_in[:, split + rope_dim:]], axis=1).astype(BF16)
    w["g_mix"] = p["g_mix"][l][None]
    w["g_cq"] = p["g_cq"][l][None]
    w["g_ckv"] = p["g_ckv"][l][None]
    pad_h = ((0, 0), (0, 0), (0, LANES - qk))
    w["w_uq"] = jnp.pad(p["w_uq"][l], pad_h).reshape(q_rank, H * LANES).astype(BF16)
    scale = np.float32(qk ** -0.5 * np.log2(np.e))
    w["g_qn"] = jnp.pad(p["g_qn"][l] * scale, (0, LANES - qk))[None]
    g_kn = jnp.pad(p["g_kn"][l], (0, LANES - qk))[None]
    w["g_kn"] = g_kn
    w["g_kn_full"] = jnp.tile(g_kn, (1, H))
    w["w_uk_pad"] = jnp.pad(p["w_uk"][l], ((0, 0), (0, 0), (0, LANES - nope))).reshape(kvr, H * LANES).astype(BF16)
    w["w_uk_il"] = jnp.swapaxes(p["w_uk"][l], 1, 2).reshape(kvr, nope * H).astype(BF16)
    w["w_uv"] = p["w_uv"][l].reshape(kvr, H * v_dim).astype(BF16)
    w["w_uv_pad"] = jnp.pad(p["w_uv"][l], ((0, 0), (0, 0), (0, LANES - v_dim))).reshape(kvr, H * LANES).astype(BF16)
    w["v_one"] = jnp.tile((jnp.arange(LANES) == v_dim).astype(F32), H)[None]
    w["conv_w"] = p["conv_w"][l]
    w["conv_b"] = p["conv_b"][l][None]
    w["w_rx"] = jnp.concatenate([p["w_ra"][l], p["w_ix"][l]], axis=2).astype(BF16)
    nb, bd = p["w_ra"].shape[1], p["w_ra"].shape[2]
    w["b_rx"] = jnp.concatenate([p["b_ra"][l].reshape(nb, 1, bd), p["b_ix"][l].reshape(nb, 1, bd)], axis=2)
    w["lam"] = p["lru_lambda"][l][None]
    for n in ("w_pa", "w_pb", "w_pc", "w_o", "w_up", "w_down"):
        w[n] = p[n][l].astype(BF16)
    w["g_ffn"] = p["g_ffn"][l][None]
    w["w_sp"] = p["w_sp"][l]
    w["b_sp"] = jnp.repeat(p["b_sp"][l].T, LANES, axis=1)
    rep = LANES // sample_rows
    w["w_sp_s"] = jnp.tile(p["w_sp"][l][:, :sample_rows, :sample_rows], (1, rep, rep))
    w["b_sp_s"] = jnp.repeat(jnp.tile(p["b_sp"][l][:, :sample_rows], (1, rep)).T, LANES, axis=1)
    return w


def _rope_tables(pos, dims):
    nope, rope_dim = dims["nope"], dims["rope_dim"]
    half = rope_dim // 2
    inv = ROPE_THETA ** (-jnp.arange(half, dtype=F32) / half)
    ang = pos.astype(F32)[:, None] * inv[None, :]
    cos, sin = jnp.cos(ang), jnp.sin(ang)
    n = pos.shape[0]
    tail = LANES - nope - rope_dim
    zero_h = jnp.zeros((n, half), F32)
    rc = jnp.concatenate([jnp.ones((n, nope), F32), cos, cos, jnp.ones((n, tail), F32)], axis=1)
    rsa = jnp.concatenate([jnp.zeros((n, nope), F32), zero_h, sin, jnp.zeros((n, tail), F32)], axis=1)
    rsb = jnp.concatenate([jnp.zeros((n, nope), F32), -sin, zero_h, jnp.zeros((n, tail), F32)], axis=1)
    return rc, rsa, rsb


def kernel(x_prompt, x_sample, cache_kv_latent, cache_k_rope, state_lru_h, state_conv, page_table,
           g_mix, w_in, g_cq, w_uq, g_ckv, w_uk, w_uv, g_qn, g_kn, w_sp, b_sp, conv_w, conv_b,
           w_ra, b_ra, w_ix, b_ix, lru_lambda, w_pa, w_pb, w_pc, w_o, g_ffn, w_up, w_down):
    p = dict(g_mix=g_mix, w_in=w_in, g_cq=g_cq, w_uq=w_uq, g_ckv=g_ckv, w_uk=w_uk, w_uv=w_uv,
             g_qn=g_qn, g_kn=g_kn, w_sp=w_sp, b_sp=b_sp, conv_w=conv_w, conv_b=conv_b, w_ra=w_ra,
             b_ra=b_ra, w_ix=w_ix, b_ix=b_ix, lru_lambda=lru_lambda, w_pa=w_pa, w_pb=w_pb, w_pc=w_pc,
             w_o=w_o, g_ffn=g_ffn, w_up=w_up, w_down=w_down)
    n_prompt, seq, d_model = x_prompt.shape
    n_dec, t_dec, _ = x_sample.shape
    depth = w_in.shape[0]
    q_rank, H, qk = w_uq.shape[1:]
    kvr, _, nope = w_uk.shape[1:]
    v_dim = w_uv.shape[3]
    rope_dim = qk - nope
    sg_w, lru_w = w_pb.shape[1], w_pc.shape[1]
    cw = conv_w.shape[1]
    page = cache_kv_latent.shape[2]
    assert n_prompt == 1 and qk <= LANES and t_dec >= cw - 1
    assert w_sp.shape[2] == LANES and w_ra.shape[2] == LANES and sg_w == w_sp.shape[1] * LANES
    assert LANES % t_dec == 0 and t_dec & (t_dec - 1) == 0

    sizes = [("cq", q_rank), ("ckv", kvr), ("kr", LANES), ("zu", sg_w), ("zv", sg_w), ("xrec", lru_w),
             ("xgate", lru_w), ("ga", d_model), ("gb", d_model), ("gc", d_model)]
    cols, off = {}, 0
    for name, n in sizes:
        cols[name] = (off, off + n)
        off += n
    dims = dict(n_heads=H, qk_dim=qk, nope=nope, rope_dim=rope_dim, v_dim=v_dim, q_rank=q_rank,
                kv_rank=kvr, lru_width=lru_w, sg_width=sg_w, lru_blocks=w_ra.shape[1], conv_width=cw,
                cols=cols, dec_seq=t_dec)

    past_len = page_table.shape[1] * page
    cache_kr_t = jnp.swapaxes(cache_k_rope, 2, 3)
    tabs_p = _rope_tables(jnp.arange(seq), dims)
    tabs_s = _rope_tables(jnp.tile(past_len + jnp.arange(t_dec), n_dec), dims)

    yp = x_prompt.reshape(seq, d_model)
    ys = x_sample.reshape(n_dec * t_dec, d_model)
    outs = {k: [] for k in ("ckv_p", "kr_p", "h_p", "cv_p", "ckv_s", "kr_s", "h_s", "cv_s", "v_s")}
    rope_sl = slice(nope, nope + rope_dim)
    for l in range(depth):
        w = _layer_weights(l, p, dims, t_dec)
        q, ckv, kr, k, v, mbc, siga, xlast, hlast = _mix_in(
            yp, tabs_p, None, w, prompt=True, seg=None, sg_seg=LANES, dims=dims)
        ya = _prompt_attention(q, k, v, n_heads=H, v_dim=v_dim)
        yp = _mix_out(yp, ya, siga, mbc, w)
        outs["ckv_p"].append(ckv[None])
        outs["kr_p"].append(kr[None, :, rope_sl])
        outs["h_p"].append(hlast[None, SUBLANES - 1])
        outs["cv_p"].append(xlast[None, SUBLANES - (cw - 1):])
        buf = state_conv[l]
        hist = jnp.stack([
            jnp.concatenate([buf[:, cw - 1 - kk:], jnp.zeros((n_dec, t_dec - kk, lru_w), F32)], axis=1)
            .reshape(n_dec * t_dec, lru_w) for kk in range(1, cw)])
        h0 = jnp.concatenate([state_lru_h[l][:, None], jnp.zeros((n_dec, t_dec - 1, lru_w), F32)],
                             axis=1).reshape(n_dec * t_dec, lru_w)
        ws = dict(w, w_sp=w["w_sp_s"], b_sp=w["b_sp_s"])
        q, ckv, kr, vsg, xrec, hseq, mbc, siga = _mix_in(
            ys, tabs_s, (hist, h0), ws, prompt=False, seg=t_dec, sg_seg=t_dec, dims=dims)
        q_rep = jnp.repeat(q.reshape(n_dec, t_dec, H * LANES), H, axis=1)
        pad_t = ((0, 0), (0, SUBLANES - t_dec), (0, 0))
        cnew = jnp.pad(ckv.reshape(n_dec, t_dec, kvr), pad_t)
        knew = jnp.pad(jnp.swapaxes(kr[:, rope_sl].reshape(n_dec, t_dec, rope_dim), 1, 2),
                       ((0, 0), (0, 0), (0, LANES - t_dec)))
        ya = _sample_attention(q_rep, cnew, knew, page_table, cache_kv_latent, cache_kr_t, w,
                               layer=l, dims=dims)
        ys = _mix_out(ys, ya.reshape(n_dec * t_dec, H * v_dim), siga, mbc, w)
        outs["ckv_s"].append(ckv.reshape(n_dec, t_dec, kvr))
        outs["kr_s"].append(kr[:, rope_sl].reshape(n_dec, t_dec, rope_dim))
        outs["h_s"].append(hseq.reshape(n_dec, t_dec, lru_w)[:, t_dec - 1])
        outs["cv_s"].append(xrec.reshape(n_dec, t_dec, lru_w)[:, t_dec - (cw - 1):])
        outs["v_s"].append(vsg.reshape(n_dec, t_dec, sg_w))

    st = {k: jnp.stack(v) for k, v in outs.items()}
    return (yp.reshape(n_prompt, seq, d_model), ys.reshape(n_dec, t_dec, d_model),
            st["ckv_p"], st["kr_p"], st["h_p"], st["cv_p"],
            st["ckv_s"], st["kr_s"], st["h_s"], st["cv_s"], st["v_s"])
```

```python
import functools

import jax
import jax.numpy as jnp
import numpy as np
from jax import lax
from jax.experimental import pallas as pl
from jax.experimental.pallas import tpu as pltpu

F32, BF16 = jnp.float32, jnp.bfloat16

EPS = 1e-6
ROPE_THETA = 10000.0
LRU_C = 8.0
NEG_INF = -1e30

LANES = 128
SUBLANES = 8
VMEM_LIMIT_BYTES = 56 * 1024 * 1024

MIX_IN_ROWS = 256
MIX_OUT_ROWS = 512
ATTN_BLOCK = 512
ATTN_ROW_CHUNK = 64
DECODE_PAGES = 128
DECODE_SUB_PAGES = 16


def _rms(x, g):
    return x * lax.rsqrt(jnp.mean(x * x, axis=-1, keepdims=True) + EPS) * g


def _gelu(x):
    c = np.float32(np.sqrt(2.0 / np.pi))
    return 0.5 * x * (1.0 + jnp.tanh(c * (x + np.float32(0.044715) * (x * x * x))))


def _dot(a, b):
    return jnp.dot(a, b, preferred_element_type=F32)


def _dot_nt(a, b):
    return lax.dot_general(a, b, (((1,), (1,)), ((), ())), preferred_element_type=F32)


def _dot_tn(a, b):
    return lax.dot_general(a, b, (((0,), (0,)), ((), ())), preferred_element_type=F32)


def _resident(shape):
    nd = len(shape)
    return pl.BlockSpec(shape, lambda *_: (0,) * nd, pipeline_mode=pl.Buffered(1))


def _rows(tm, width):
    return pl.BlockSpec((tm, width), lambda i: (i, 0))


def _mix_in_kernel(*refs, prompt, tm, seg, sg_seg, cols, n_heads, qk_dim, rope_half, n_blocks,
                   conv_width):
    it = iter(refs)
    x_ref, rc_ref, rsa_ref, rsb_ref = next(it), next(it), next(it), next(it)
    if not prompt:
        hist_ref, h0_ref = next(it), next(it)
    gmix_ref, win_ref, gcq_ref, wuq_ref, gqn_ref, gckv_ref = (next(it) for _ in range(6))
    if prompt:
        wuk_ref, gkn_ref, wuv_ref, vone_ref = next(it), next(it), next(it), next(it)
    (wsp_ref, bsp_ref, convw_ref, convb_ref, wrx_ref, brx_ref, lam_ref, wpb_ref,
     wpc_ref) = (next(it) for _ in range(9))
    q_ref, ckv_ref, kr_ref = next(it), next(it), next(it)
    if prompt:
        k_ref, v_ref = next(it), next(it)
    else:
        vsg_ref, xrec_ref, hseq_ref = next(it), next(it), next(it)
    mbc_ref, siga_ref = next(it), next(it)
    if prompt:
        xlast_ref, hlast_ref = next(it), next(it)
    yb_scr = next(it)
    if prompt:
        xprev_scr, hcarry_scr = next(it), next(it)

        @pl.when(pl.program_id(0) == 0)
        def _():
            xprev_scr[...] = jnp.zeros_like(xprev_scr)
            hcarry_scr[...] = jnp.zeros_like(hcarry_scr)

    h = _rms(x_ref[...], gmix_ref[...]).astype(BF16)

    def proj(name):
        lo, hi = cols[name]
        return _dot(h, win_ref[:, lo:hi])

    rc, rsa, rsb = rc_ref[...], rsa_ref[...], rsb_ref[...]

    def rope(blk):
        return (blk * rc + pltpu.roll(blk, rope_half, 1) * rsa
                + pltpu.roll(blk, LANES - rope_half, 1) * rsb)

    def head_norm(blk, g):
        ssq = jnp.sum(blk * blk, axis=-1, keepdims=True)
        return blk * lax.rsqrt(ssq * np.float32(1.0 / qk_dim) + EPS) * g

    cq = _rms(proj("cq"), gcq_ref[...]).astype(BF16)
    qf = _dot(cq, wuq_ref[...])
    for hd in range(n_heads):
        sl = slice(hd * LANES, (hd + 1) * LANES)
        q_ref[:, sl] = head_norm(rope(qf[:, sl]), gqn_ref[...]).astype(BF16)
    ckv = _rms(proj("ckv"), gckv_ref[...])
    ckv_ref[...] = ckv
    kr = rope(proj("kr"))
    kr_ref[...] = kr
    if prompt:
        ckv_bf = ckv.astype(BF16)
        kn = _dot(ckv_bf, wuk_ref[...])
        for hd in range(n_heads):
            sl = slice(hd * LANES, (hd + 1) * LANES)
            k_ref[:, sl] = head_norm(kn[:, sl] + kr, gkn_ref[...]).astype(BF16)
        v_ref[...] = (_dot(ckv_bf, wuv_ref[...]) + vone_ref[...]).astype(BF16)

    u = _gelu(proj("zu"))
    vs = _gelu(proj("zv"))
    if not prompt:
        vsg_ref[...] = vs
    r_i = lax.broadcasted_iota(jnp.int32, (LANES, LANES), 0)
    c_i = lax.broadcasted_iota(jnp.int32, (LANES, LANES), 1)
    keep = c_i <= r_i
    if sg_seg < LANES:
        shift = sg_seg.bit_length() - 1
        keep = keep & ((r_i >> shift) == (c_i >> shift))
    n_groups = wsp_ref.shape[0]
    wm = [jnp.where(keep, wsp_ref[g], 0.0).astype(BF16) for g in range(n_groups)]
    for c in range(tm // LANES):
        rs = slice(c * LANES, (c + 1) * LANES)
        for g in range(n_groups):
            cs = slice(g * LANES, (g + 1) * LANES)
            mix = _dot(wm[g], vs[rs, cs].astype(BF16)) + bsp_ref[:, cs]
            yb_scr[rs, cs] = (u[rs, cs] * mix).astype(BF16)

    xr = proj("xrec")
    width = xr.shape[1]
    row = lax.broadcasted_iota(jnp.int32, (tm, width), 0)
    t_idx = row & (seg - 1)
    xc = convb_ref[...] + xr * convw_ref[conv_width - 1:conv_width, :]
    for k in range(1, conv_width):
        if prompt:
            fill = pltpu.roll(xprev_scr[...], k, 0)
        else:
            fill = hist_ref[k - 1]
        xs = jnp.where(t_idx >= k, pltpu.roll(xr, k, 0), fill)
        xc = xc + xs * convw_ref[conv_width - 1 - k:conv_width - k, :]
    if prompt:
        xprev_scr[...] = xr
        xlast_ref[...] = xr[tm - SUBLANES:, :]
    else:
        xrec_ref[...] = xr

    lam = lam_ref[...]
    neg_sp = -LRU_C * (jnp.maximum(-lam, 0.0) + jnp.log1p(jnp.exp(-jnp.abs(lam))))
    a_parts, b_parts = [], []
    for n in range(n_blocks):
        sl = slice(n * LANES, (n + 1) * LANES)
        xcb = xc[:, sl]
        ri = _dot(xcb.astype(BF16), wrx_ref[n]) + brx_ref[n]
        r_gate = jax.nn.sigmoid(ri[:, :LANES])
        i_gate = jax.nn.sigmoid(ri[:, LANES:])
        log_a = r_gate * neg_sp[:, sl]
        a_blk = jnp.exp(log_a)
        a_parts.append(a_blk)
        b_parts.append(jnp.sqrt(jnp.tanh(-log_a) * (1.0 + a_blk * a_blk)) * (i_gate * xcb))
    a = jnp.concatenate(a_parts, axis=1)
    b = jnp.concatenate(b_parts, axis=1)
    if prompt:
        h0 = jnp.where(row == 0, hcarry_scr[0:1, :], 0.0)
    else:
        h0 = h0_ref[...]
    b = b + a * h0
    s = 1
    while s < seg:
        ok = t_idx >= s
        a_sh = jnp.where(ok, pltpu.roll(a, s, 0), 1.0)
        b_sh = jnp.where(ok, pltpu.roll(b, s, 0), 0.0)
        b = a * b_sh + b
        a = a * a_sh
        s *= 2
    hseq = b
    if prompt:
        hcarry_scr[...] = jnp.broadcast_to(hseq[tm - 1:tm, :], hcarry_scr.shape)
        hlast_ref[...] = hseq[tm - SUBLANES:, :]
    else:
        hseq_ref[...] = hseq
    yc = (hseq * _gelu(proj("xgate"))).astype(BF16)

    mb = jax.nn.sigmoid(proj("gb")) * _dot(yb_scr[...], wpb_ref[...])
    mc = jax.nn.sigmoid(proj("gc")) * _dot(yc, wpc_ref[...])
    mbc_ref[...] = mb + mc
    siga_ref[...] = jax.nn.sigmoid(proj("ga"))


def _mix_in(x, tabs, state, wts, *, prompt, seg, sg_seg, dims):
    rows, d_model = x.shape
    tm = min(MIX_IN_ROWS, rows)
    assert rows % tm == 0 and tm % LANES == 0
    if prompt:
        seg = tm
    assert seg & (seg - 1) == 0 and sg_seg & (sg_seg - 1) == 0 and tm % seg == 0
    H, kvr, lru_w, sg_w = dims["n_heads"], dims["kv_rank"], dims["lru_width"], dims["sg_width"]
    hp = H * LANES
    kern = functools.partial(
        _mix_in_kernel, prompt=prompt, tm=tm, seg=seg, sg_seg=sg_seg, cols=dims["cols"],
        n_heads=H, qk_dim=dims["qk_dim"], rope_half=dims["rope_dim"] // 2,
        n_blocks=dims["lru_blocks"], conv_width=dims["conv_width"])

    ins = [x, *tabs]
    in_specs = [_rows(tm, d_model)] + [_rows(tm, LANES)] * 3
    if not prompt:
        hist, h0 = state
        ins += [hist, h0]
        in_specs += [pl.BlockSpec((hist.shape[0], tm, lru_w), lambda i: (0, i, 0)), _rows(tm, lru_w)]
    names = ["g_mix", "w_in", "g_cq", "w_uq", "g_qn", "g_ckv"]
    if prompt:
        names += ["w_uk_pad", "g_kn", "w_uv_pad", "v_one"]
    names += ["w_sp", "b_sp", "conv_w", "conv_b", "w_rx", "b_rx", "lam", "w_pb", "w_pc"]
    for n in names:
        ins.append(wts[n])
        in_specs.append(_resident(wts[n].shape))

    out_shape = [jax.ShapeDtypeStruct((rows, hp), BF16),
                 jax.ShapeDtypeStruct((rows, kvr), F32),
                 jax.ShapeDtypeStruct((rows, LANES), F32)]
    out_specs = [_rows(tm, hp), _rows(tm, kvr), _rows(tm, LANES)]
    if prompt:
        out_shape += [jax.ShapeDtypeStruct((rows, hp), BF16),
                      jax.ShapeDtypeStruct((rows, hp), BF16)]
        out_specs += [_rows(tm, hp), _rows(tm, hp)]
    else:
        out_shape += [jax.ShapeDtypeStruct((rows, sg_w), F32),
                      jax.ShapeDtypeStruct((rows, lru_w), F32),
                      jax.ShapeDtypeStruct((rows, lru_w), F32)]
        out_specs += [_rows(tm, sg_w), _rows(tm, lru_w), _rows(tm, lru_w)]
    out_shape += [jax.ShapeDtypeStruct((rows, d_model), F32)] * 2
    out_specs += [_rows(tm, d_model)] * 2
    scratch = [pltpu.VMEM((tm, sg_w), BF16)]
    if prompt:
        last = pl.BlockSpec((SUBLANES, lru_w), lambda i: (0, 0))
        out_shape += [jax.ShapeDtypeStruct((SUBLANES, lru_w), F32)] * 2
        out_specs += [last, last]
        scratch += [pltpu.VMEM((tm, lru_w), F32), pltpu.VMEM((SUBLANES, lru_w), F32)]

    return pl.pallas_call(
        kern, grid=(rows // tm,), in_specs=in_specs, out_specs=out_specs, out_shape=out_shape,
        scratch_shapes=scratch, name="mix_in_prompt" if prompt else "mix_in_sample",
        compiler_params=pltpu.CompilerParams(dimension_semantics=("arbitrary",),
                                             vmem_limit_bytes=VMEM_LIMIT_BYTES),
    )(*ins)


def _flash_kernel(qi_ref, ki_ref, q_ref, k_ref, v_ref, o_ref, m_scr, acc_scr, s_scr, p_scr,
                  alpha_scr, *, n_heads, v_dim, blk, rc):
    step = pl.program_id(0)
    qi, ki = qi_ref[step], ki_ref[step]
    n_lt = blk // LANES

    @pl.when(ki == 0)
    def _():
        m_scr[...] = jnp.full_like(m_scr, NEG_INF)
        acc_scr[...] = jnp.zeros_like(acc_scr)

    def head_cols(hd):
        return slice(hd * LANES, (hd + 1) * LANES)

    def qk(hd):
        s_scr[hd % 2] = _dot_nt(q_ref[:, head_cols(hd)], k_ref[:, head_cols(hd)])

    def update(masked):
        qk(0)
        for hd in range(n_heads):
            if hd + 1 < n_heads:
                qk(hd + 1)
            sb = hd % 2
            for r0 in range(0, blk, rc):
                rows = slice(r0, r0 + rc)
                tiles = [s_scr[sb, rows, j * LANES:(j + 1) * LANES] for j in range(n_lt)]
                if masked:
                    r_i = lax.broadcasted_iota(jnp.int32, (rc, LANES), 0) + r0
                    c_i = lax.broadcasted_iota(jnp.int32, (rc, LANES), 1)
                    tiles = [jnp.where(c_i + j * LANES <= r_i, t, NEG_INF) for j, t in enumerate(tiles)]
                row_max = jnp.max(functools.reduce(jnp.maximum, tiles), axis=-1, keepdims=True)
                m_prev = m_scr[hd, rows, :]
                m_new = jnp.maximum(m_prev, jnp.broadcast_to(row_max, (rc, LANES)))
                alpha_scr[sb, rows, :] = jnp.exp2(m_prev - m_new)
                m_scr[hd, rows, :] = m_new
                for j, t in enumerate(tiles):
                    p_scr[sb, rows, j * LANES:(j + 1) * LANES] = jnp.exp2(t - m_new).astype(BF16)
            acc_scr[hd] = alpha_scr[sb] * acc_scr[hd] + _dot(p_scr[sb], v_ref[:, head_cols(hd)])

    @pl.when(ki < qi)
    def _():
        update(False)

    @pl.when(ki == qi)
    def _():
        update(True)
        for hd in range(n_heads):
            acc = acc_scr[hd]
            o_ref[:, hd * v_dim:(hd + 1) * v_dim] = (
                acc[:, :v_dim] / acc[:, v_dim:v_dim + 1]).astype(o_ref.dtype)


def _prompt_attention(q, k, v, *, n_heads, v_dim):
    rows = q.shape[0]
    blk = min(ATTN_BLOCK, rows)
    assert rows % blk == 0
    nb = rows // blk
    pairs = [(i, j) for i in range(nb) for j in range(i + 1)]
    qi = jnp.asarray([p[0] for p in pairs], jnp.int32)
    ki = jnp.asarray([p[1] for p in pairs], jnp.int32)
    grid_spec = pltpu.PrefetchScalarGridSpec(
        num_scalar_prefetch=2, grid=(len(pairs),),
        in_specs=[pl.BlockSpec((blk, q.shape[1]), lambda s, qi, ki: (qi[s], 0)),
                  pl.BlockSpec((blk, k.shape[1]), lambda s, qi, ki: (ki[s], 0)),
                  pl.BlockSpec((blk, v.shape[1]), lambda s, qi, ki: (ki[s], 0))],
        out_specs=pl.BlockSpec((blk, n_heads * v_dim), lambda s, qi, ki: (qi[s], 0)),
        scratch_shapes=[pltpu.VMEM((n_heads, blk, LANES), F32),
                        pltpu.VMEM((n_heads, blk, LANES), F32),
                        pltpu.VMEM((2, blk, blk), F32),
                        pltpu.VMEM((2, blk, blk), BF16),
                        pltpu.VMEM((2, blk, LANES), F32)])
    return pl.pallas_call(
        functools.partial(_flash_kernel, n_heads=n_heads, v_dim=v_dim, blk=blk,
                          rc=min(ATTN_ROW_CHUNK, blk)),
        grid_spec=grid_spec, out_shape=jax.ShapeDtypeStruct((rows, n_heads * v_dim), BF16),
        name="prompt_attention",
        compiler_params=pltpu.CompilerParams(dimension_semantics=("arbitrary",),
                                             vmem_limit_bytes=VMEM_LIMIT_BYTES),
    )(qi, ki, q, k, v)


def _decode_kernel(pt_ref, q_ref, cnew_ref, knew_ref, gkn_ref, wukp_ref, wuk_ref, wuv_ref,
                   ckv_hbm, kr_hbm, o_ref, ckv_buf, kr_buf, sem, qabs_scr, qpe_scr, m_scr, l_scr,
                   acc_scr, *, layer, n_pages, n_chunks, sub_pages, n_heads, n_q, nope, rope_dim,
                   qk_dim, v_dim):
    b, c = pl.program_id(0), pl.program_id(1)
    n_b = pl.num_programs(0)
    step = b * n_chunks + c
    slot = step % 2
    nqh = n_q * n_heads

    def chunk_copies(bb, cc, sl):
        out = []
        for j in range(n_pages):
            pid = pt_ref[bb, cc * n_pages + j]
            out.append(pltpu.make_async_copy(ckv_hbm.at[layer, pid], ckv_buf.at[sl, j], sem.at[0, sl]))
            out.append(pltpu.make_async_copy(kr_hbm.at[layer, pid], kr_buf.at[sl, j], sem.at[1, sl]))
        return out

    @pl.when(step == 0)
    def _():
        for d in chunk_copies(0, 0, 0):
            d.start()

    @pl.when(step + 1 < n_b * n_chunks)
    def _():
        nxt = step + 1
        for d in chunk_copies(nxt // n_chunks, nxt % n_chunks, 1 - slot):
            d.start()

    head_sel = ((lax.broadcasted_iota(jnp.int32, (nqh, LANES), 0) & (n_heads - 1))
                == (lax.broadcasted_iota(jnp.int32, (nqh, LANES), 1) & (n_heads - 1))).astype(BF16)

    def scores(ckv, kpe_t):
        ckv_bf = ckv.astype(BF16)
        kn = _dot(ckv_bf, wuk_ref[...])
        sq = kn * kn
        t = sq[:, 0:LANES]
        for j in range(1, sq.shape[1] // LANES):
            t = t + sq[:, j * LANES:(j + 1) * LANES]
        ssq = _dot_nt(head_sel, t.astype(BF16)) + jnp.sum(kpe_t * kpe_t, axis=0, keepdims=True)
        rs = lax.rsqrt(ssq * np.float32(1.0 / qk_dim) + EPS)
        s = _dot(qabs_scr[...], ckv.T.astype(BF16)) + _dot(qpe_scr[...], kpe_t.astype(BF16))
        return s * rs, ckv_bf

    def accumulate(parts):
        m_prev = m_scr[...]
        m_new = m_prev
        for s, _ in parts:
            m_new = jnp.maximum(m_new, jnp.max(s, axis=1, keepdims=True))
        alpha = jnp.exp2(m_prev - m_new)
        l_new = alpha * l_scr[...]
        acc = alpha * acc_scr[...]
        for s, ckv_bf in parts:
            p = jnp.exp2(s - m_new)
            l_new = l_new + jnp.sum(p, axis=1, keepdims=True)
            acc = acc + _dot(p.astype(BF16), ckv_bf)
        l_scr[...] = l_new
        acc_scr[...] = acc
        m_scr[...] = m_new

    @pl.when(c == 0)
    def _():
        qg = q_ref[0].astype(F32) * gkn_ref[...]
        r_i = lax.broadcasted_iota(jnp.int32, qg.shape, 0)
        c_i = lax.broadcasted_iota(jnp.int32, qg.shape, 1)
        qbd = jnp.where((c_i >> 7) == (r_i & (n_heads - 1)), qg, 0.0)
        qabs_scr[...] = _dot_nt(qbd.astype(BF16), wukp_ref[...]).astype(BF16)
        qfold = qbd[:, 0:LANES]
        for hd in range(1, n_heads):
            qfold = qfold + qbd[:, hd * LANES:(hd + 1) * LANES]
        qpe_scr[...] = qfold[:, nope:nope + rope_dim].astype(BF16)
        m_scr[...] = jnp.full_like(m_scr, NEG_INF)
        l_scr[...] = jnp.zeros_like(l_scr)
        acc_scr[...] = jnp.zeros_like(acc_scr)
        cnew = cnew_ref[0]
        cnew = jnp.concatenate([cnew, jnp.zeros((LANES - cnew.shape[0], cnew.shape[1]), F32)], axis=0)
        s, ckv_bf = scores(cnew, knew_ref[0])
        j_i = lax.broadcasted_iota(jnp.int32, s.shape, 1)
        q_i = lax.broadcasted_iota(jnp.int32, s.shape, 0) >> (n_heads.bit_length() - 1)
        s = jnp.where((j_i < n_q) & (j_i <= q_i), s, NEG_INF)
        accumulate([(s, ckv_bf)])

    for d in chunk_copies(b, c, slot):
        d.wait()

    page = ckv_buf.shape[2]
    parts = []
    for j0 in range(0, n_pages, sub_pages):
        ckv = ckv_buf[slot, j0:j0 + sub_pages].reshape(sub_pages * page, ckv_buf.shape[3])
        kpe_t = jnp.concatenate([kr_buf[slot, j] for j in range(j0, j0 + sub_pages)], axis=1)
        parts.append(scores(ckv, kpe_t))
    accumulate(parts)

    @pl.when(c == n_chunks - 1)
    def _():
        lat = acc_scr[...] / l_scr[...]
        yfull = _dot(lat.astype(BF16), wuv_ref[...])
        r_i = lax.broadcasted_iota(jnp.int32, yfull.shape, 0)
        c_i = lax.broadcasted_iota(jnp.int32, yfull.shape, 1)
        own = (c_i >> (v_dim.bit_length() - 1)) == (r_i & (n_heads - 1))
        y = jnp.where(own, yfull, 0.0).reshape(n_q, n_heads, yfull.shape[1]).sum(axis=1)
        o_ref[0] = y.astype(o_ref.dtype)


def _sample_attention(q_rep, cnew, knew, page_table, cache_kv, cache_kr, wts, *, layer, dims):
    n_b, n_tab = page_table.shape
    n_pages = min(DECODE_PAGES, n_tab)
    sub_pages = min(DECODE_SUB_PAGES, n_pages)
    assert n_tab % n_pages == 0 and n_pages % sub_pages == 0
    n_chunks = n_tab // n_pages
    H, n_q, v_dim = dims["n_heads"], dims["dec_seq"], dims["v_dim"]
    nqh = n_q * H
    assert H == SUBLANES and n_q <= SUBLANES and LANES == 1 << 7
    assert v_dim & (v_dim - 1) == 0
    page, kvr, rope_dim = cache_kv.shape[2], cache_kv.shape[3], cache_kr.shape[2]
    assert (H * dims["nope"]) % LANES == 0 and LANES % H == 0
    kern = functools.partial(
        _decode_kernel, layer=layer, n_pages=n_pages, n_chunks=n_chunks, sub_pages=sub_pages,
        n_heads=H, n_q=n_q, nope=dims["nope"], rope_dim=rope_dim, qk_dim=dims["qk_dim"], v_dim=v_dim)

    def per_b(shape):
        nd = len(shape)
        return pl.BlockSpec((1,) + tuple(shape[1:]), lambda b, c, pt: (b,) + (0,) * (nd - 1))

    def const(shape):
        nd = len(shape)
        return pl.BlockSpec(shape, lambda b, c, pt: (0,) * nd, pipeline_mode=pl.Buffered(1))

    weights = [wts["g_kn_full"], wts["w_uk_pad"], wts["w_uk_il"], wts["w_uv"]]
    grid_spec = pltpu.PrefetchScalarGridSpec(
        num_scalar_prefetch=1, grid=(n_b, n_chunks),
        in_specs=[per_b(q_rep.shape), per_b(cnew.shape), per_b(knew.shape)]
                 + [const(w.shape) for w in weights]
                 + [pl.BlockSpec(memory_space=pl.ANY), pl.BlockSpec(memory_space=pl.ANY)],
        out_specs=pl.BlockSpec((1, n_q, H * v_dim), lambda b, c, pt: (b, 0, 0)),
        scratch_shapes=[pltpu.VMEM((2, n_pages, page, kvr), F32),
                        pltpu.VMEM((2, n_pages, rope_dim, page), F32),
                        pltpu.SemaphoreType.DMA((2, 2)),
                        pltpu.VMEM((nqh, kvr), BF16), pltpu.VMEM((nqh, rope_dim), BF16),
                        pltpu.VMEM((nqh, 1), F32), pltpu.VMEM((nqh, 1), F32),
                        pltpu.VMEM((nqh, kvr), F32)])
    return pl.pallas_call(
        kern, grid_spec=grid_spec,
        out_shape=jax.ShapeDtypeStruct((n_b, n_q, H * v_dim), F32), name="sample_attention",
        compiler_params=pltpu.CompilerParams(dimension_semantics=("arbitrary", "arbitrary"),
                                             vmem_limit_bytes=VMEM_LIMIT_BYTES),
    )(page_table, q_rep, cnew, knew, *weights, cache_kv, cache_kr)


def _mix_out_kernel(x_ref, ya_ref, siga_ref, mbc_ref, wpa_ref, wo_ref, gffn_ref, wup_ref, wdn_ref,
                    o_ref, *, ff_chunk):
    merged = siga_ref[...] * _dot(ya_ref[...].astype(BF16), wpa_ref[...]) + mbc_ref[...]
    x1 = x_ref[...] + _dot(merged.astype(BF16), wo_ref[...])
    hf = _rms(x1, gffn_ref[...]).astype(BF16)
    acc = x1
    for lo in range(0, wup_ref.shape[1], ff_chunk):
        up = jnp.maximum(_dot(hf, wup_ref[:, lo:lo + ff_chunk]), 0.0)
        acc = acc + _dot((up * up).astype(BF16), wdn_ref[lo:lo + ff_chunk, :])
    o_ref[...] = acc


def _mix_out(x, ya, siga, mbc, wts):
    rows, d_model = x.shape
    tm = min(MIX_OUT_ROWS, rows)
    assert rows % tm == 0
    names = ["w_pa", "w_o", "g_ffn", "w_up", "w_down"]
    ff = wts["w_up"].shape[1]
    return pl.pallas_call(
        functools.partial(_mix_out_kernel, ff_chunk=min(ff, 1024)),
        grid=(rows // tm,),
        in_specs=[_rows(tm, d_model), _rows(tm, ya.shape[1]), _rows(tm, d_model), _rows(tm, d_model)]
                 + [_resident(wts[n].shape) for n in names],
        out_specs=_rows(tm, d_model), out_shape=jax.ShapeDtypeStruct((rows, d_model), F32),
        name="mix_out",
        compiler_params=pltpu.CompilerParams(dimension_semantics=("arbitrary",),
                                             vmem_limit_bytes=VMEM_LIMIT_BYTES),
    )(x, ya, siga, mbc, *[wts[n] for n in names])


def _layer_weights(l, p, dims, sample_rows):
    H, qk, nope, rope_dim, v_dim = (dims[k] for k in ("n_heads", "qk_dim", "nope", "rope_dim", "v_dim"))
    q_rank, kvr = dims["q_rank"], dims["kv_rank"]
    d_model = p["w_in"].shape[1]
    w_in = p["w_in"][l]
    split = q_rank + kvr
    w_kr = jnp.zeros((d_model, LANES), F32).at[:, nope:nope + rope_dim].set(w_in[:, split:split + rope_dim])
    w = {}
    w["w_in"] = jnp.concatenate([w_in[:, :split], w_kr, w_in[:, split + rope_dim:]], axis=1).astype(BF16)
    w["g_mix"] = p["g_mix"][l][None]
    w["g_cq"] = p["g_cq"][l][None]
    w["g_ckv"] = p["g_ckv"][l][None]
    pad_h = ((0, 0), (0, 0), (0, LANES - qk))
    w["w_uq"] = jnp.pad(p["w_uq"][l], pad_h).reshape(q_rank, H * LANES).astype(BF16)
    scale = np.float32(qk ** -0.5 * np.log2(np.e))
    w["g_qn"] = jnp.pad(p["g_qn"][l] * scale, (0, LANES - qk))[None]
    g_kn = jnp.pad(p["g_kn"][l], (0, LANES - qk))[None]
    w["g_kn"] = g_kn
    w["g_kn_full"] = jnp.tile(g_kn, (1, H))
    w["w_uk_pad"] = jnp.pad(p["w_uk"][l], ((0, 0), (0, 0), (0, LANES - nope))).reshape(kvr, H * LANES).astype(BF16)
    w["w_uk_il"] = jnp.swapaxes(p["w_uk"][l], 1, 2).reshape(kvr, nope * H).astype(BF16)
    w["w_uv"] = p["w_uv"][l].reshape(kvr, H * v_dim).astype(BF16)
    w["w_uv_pad"] = jnp.pad(p["w_uv"][l], ((0, 0), (0, 0), (0, LANES - v_dim))).reshape(kvr, H * LANES).astype(BF16)
    w["v_one"] = jnp.tile((jnp.arange(LANES) == v_dim).astype(F32), H)[None]
    w["conv_w"] = p["conv_w"][l]
    w["conv_b"] = p["conv_b"][l][None]
    w["w_rx"] = jnp.concatenate([p["w_ra"][l], p["w_ix"][l]], axis=2).astype(BF16)
    nb, bd = p["w_ra"].shape[1], p["w_ra"].shape[2]
    w["b_rx"] = jnp.concatenate([p["b_ra"][l].reshape(nb, 1, bd), p["b_ix"][l].reshape(nb, 1, bd)], axis=2)
    w["lam"] = p["lru_lambda"][l][None]
    for n in ("w_pa", "w_pb", "w_pc", "w_o", "w_up", "w_down"):
        w[n] = p[n][l].astype(BF16)
    w["g_ffn"] = p["g_ffn"][l][None]
    w["w_sp"] = p["w_sp"][l]
    w["b_sp"] = jnp.repeat(p["b_sp"][l].T, LANES, axis=1)
    rep = LANES // sample_rows
    w["w_sp_s"] = jnp.tile(p["w_sp"][l][:, :sample_rows, :sample_rows], (1, rep, rep))
    w["b_sp_s"] = jnp.repeat(jnp.tile(p["b_sp"][l][:, :sample_rows], (1, rep)).T, LANES, axis=1)
    return w


def _rope_tables(pos, dims):
    nope, rope_dim = dims["nope"], dims["rope_dim"]
    half = rope_dim // 2
    inv = ROPE_THETA ** (-jnp.arange(half, dtype=F32) / half)
    ang = pos.astype(F32)[:, None] * inv[None, :]
    cos, sin = jnp.cos(ang), jnp.sin(ang)
    n = pos.shape[0]
    tail = LANES - nope - rope_dim
    zero_h = jnp.zeros((n, half), F32)
    rc = jnp.concatenate([jnp.ones((n, nope), F32), cos, cos, jnp.ones((n, tail), F32)], axis=1)
    rsa = jnp.concatenate([jnp.zeros((n, nope), F32), zero_h, sin, jnp.zeros((n, tail), F32)], axis=1)
    rsb = jnp.concatenate([jnp.zeros((n, nope), F32), -sin, zero_h, jnp.zeros((n, tail), F32)], axis=1)
    return rc, rsa, rsb


def kernel(x_prompt, x_sample, cache_kv_latent, cache_k_rope, state_lru_h, state_conv, page_table,
           g_mix, w_in, g_cq, w_uq, g_ckv, w_uk, w_uv, g_qn, g_kn, w_sp, b_sp, conv_w, conv_b,
           w_ra, b_ra, w_ix, b_ix, lru_lambda, w_pa, w_pb, w_pc, w_o, g_ffn, w_up, w_down):
    p = dict(g_mix=g_mix, w_in=w_in, g_cq=g_cq, w_uq=w_uq, g_ckv=g_ckv, w_uk=w_uk, w_uv=w_uv,
             g_qn=g_qn, g_kn=g_kn, w_sp=w_sp, b_sp=b_sp, conv_w=conv_w, conv_b=conv_b, w_ra=w_ra,
             b_ra=b_ra, w_ix=w_ix, b_ix=b_ix, lru_lambda=lru_lambda, w_pa=w_pa, w_pb=w_pb, w_pc=w_pc,
             w_o=w_o, g_ffn=g_ffn, w_up=w_up, w_down=w_down)
    n_prompt, seq, d_model = x_prompt.shape
    n_dec, t_dec, _ = x_sample.shape
    depth = w_in.shape[0]
    q_rank, H, qk = w_uq.shape[1:]
    kvr, _, nope = w_uk.shape[1:]
    v_dim = w_uv.shape[3]
    rope_dim = qk - nope
    sg_w, lru_w = w_pb.shape[1], w_pc.shape[1]
    cw = conv_w.shape[1]
    page = cache_kv_latent.shape[2]
    assert n_prompt == 1 and qk <= LANES and t_dec >= cw - 1
    assert w_sp.shape[2] == LANES and w_ra.shape[2] == LANES and sg_w == w_sp.shape[1] * LANES
    assert LANES % t_dec == 0 and t_dec & (t_dec - 1) == 0

    sizes = [("cq", q_rank), ("ckv", kvr), ("kr", LANES), ("zu", sg_w), ("zv", sg_w), ("xrec", lru_w),
             ("xgate", lru_w), ("ga", d_model), ("gb", d_model), ("gc", d_model)]
    cols, off = {}, 0
    for name, n in sizes:
        cols[name] = (off, off + n)
        off += n
    dims = dict(n_heads=H, qk_dim=qk, nope=nope, rope_dim=rope_dim, v_dim=v_dim, q_rank=q_rank,
                kv_rank=kvr, lru_width=lru_w, sg_width=sg_w, lru_blocks=w_ra.shape[1], conv_width=cw,
                cols=cols, dec_seq=t_dec)

    past_len = page_table.shape[1] * page
    cache_kr_t = jnp.swapaxes(cache_k_rope, 2, 3)
    tabs_p = _rope_tables(jnp.arange(seq), dims)
    tabs_s = _rope_tables(jnp.tile(past_len + jnp.arange(t_dec), n_dec), dims)

    yp = x_prompt.reshape(seq, d_model)
    ys = x_sample.reshape(n_dec * t_dec, d_model)
    outs = {k: [] for k in ("ckv_p", "kr_p", "h_p", "cv_p", "ckv_s", "kr_s", "h_s", "cv_s", "v_s")}
    rope_sl = slice(nope, nope + rope_dim)
    for l in range(depth):
        w = _layer_weights(l, p, dims, t_dec)
        q, ckv, kr, k, v, mbc, siga, xlast, hlast = _mix_in(
            yp, tabs_p, None, w, prompt=True, seg=None, sg_seg=LANES, dims=dims)
        ya = _prompt_attention(q, k, v, n_heads=H, v_dim=v_dim)
        yp = _mix_out(yp, ya, siga, mbc, w)
        outs["ckv_p"].append(ckv[None])
        outs["kr_p"].append(kr[None, :, rope_sl])
        outs["h_p"].append(hlast[None, SUBLANES - 1])
        outs["cv_p"].append(xlast[None, SUBLANES - (cw - 1):])
        buf = state_conv[l]
        hist = jnp.stack([
            jnp.concatenate([buf[:, cw - 1 - kk:], jnp.zeros((n_dec, t_dec - kk, lru_w), F32)], axis=1)
            .reshape(n_dec * t_dec, lru_w) for kk in range(1, cw)])
        h0 = jnp.concatenate([state_lru_h[l][:, None], jnp.zeros((n_dec, t_dec - 1, lru_w), F32)],
                             axis=1).reshape(n_dec * t_dec, lru_w)
        ws = dict(w, w_sp=w["w_sp_s"], b_sp=w["b_sp_s"])
        q, ckv, kr, vsg, xrec, hseq, mbc, siga = _mix_in(
            ys, tabs_s, (hist, h0), ws, prompt=False, seg=t_dec, sg_seg=t_dec, dims=dims)
        q_rep = jnp.repeat(q.reshape(n_dec, t_dec, H * LANES), H, axis=1)
        pad_t = ((0, 0), (0, SUBLANES - t_dec), (0, 0))
        cnew = jnp.pad(ckv.reshape(n_dec, t_dec, kvr), pad_t)
        knew = jnp.pad(jnp.swapaxes(kr[:, rope_sl].reshape(n_dec, t_dec, rope_dim), 1, 2),
                       ((0, 0), (0, 0), (0, LANES - t_dec)))
        ya = _sample_attention(q_rep, cnew, knew, page_table, cache_kv_latent, cache_kr_t, w,
                               layer=l, dims=dims)
        ys = _mix_out(ys, ya.reshape(n_dec * t_dec, H * v_dim), siga, mbc, w)
        outs["ckv_s"].append(ckv.reshape(n_dec, t_dec, kvr))
        outs["kr_s"].append(kr[:, rope_sl].reshape(n_dec, t_dec, rope_dim))
        outs["h_s"].append(hseq.reshape(n_dec, t_dec, lru_w)[:, t_dec - 1])
        outs["cv_s"].append(xrec.reshape(n_dec, t_dec, lru_w)[:, t_dec - (cw - 1):])
        outs["v_s"].append(vsg.reshape(n_dec, t_dec, sg_w))

    st = {k: jnp.stack(v) for k, v in outs.items()}
    return (yp.reshape(n_prompt, seq, d_model), ys.reshape(n_dec, t_dec, d_model),
            st["ckv_p"], st["kr_p"], st["h_p"], st["cv_p"],
            st["ckv_s"], st["kr_s"], st["h_s"], st["cv_s"], st["v_s"])
```
